```python
import math
import jax, jax.numpy as jnp
from jax import lax
import numpy as np

D_MODEL = 1024
BATCH = 32
SEQ = 2048
DEPTH = 1

D_MIX = D_MODEL
ATTN_HEADS = 8
ATTN_HD = 64
ATTN_W = ATTN_HEADS * ATTN_HD
MOBA_BLOCK = 256
MOBA_TOPK = 3
Q_BLOCK = 128
REC_HEADS = 4
REC_DK = 128
REC_DV = 128
REC_W = REC_HEADS * REC_DK
REC_VW = REC_HEADS * REC_DV
REC_CHUNK = 64
PEER_HEADS = 8
PEER_NKEYS = 128
PEER_N = PEER_NKEYS * PEER_NKEYS
PEER_DKEY = 256
PEER_TOPK = 16
PEER_TOKEN_BLOCK = 128
NORM_EPS = 1e-6
IN_WIDTHS = [ATTN_W, ATTN_W, ATTN_W, REC_W, REC_W, REC_VW, REC_VW]
IN_COLS = sum(IN_WIDTHS)
IN_SPLITS = [int(c) for c in np.cumsum(IN_WIDTHS)[:-1]]

kernel_name = 'hymba_moba_hgrn2_peer_block'


def rmsnorm(x, g):
    xf = x.astype(jnp.float32)
    y = xf * lax.rsqrt(jnp.mean(xf * xf, axis=-1, keepdims=True) + NORM_EPS)
    return (y * g.astype(jnp.float32)).astype(x.dtype)


def alibi_slopes(n):
    return 2.0 ** (-8.0 * jnp.arange(1, n + 1, dtype=jnp.float32) / n)


def moba_attention(q, k, v):
    B, S, H, hd = q.shape
    nb = -(-S // MOBA_BLOCK)
    pad = nb * MOBA_BLOCK - S
    kp = jnp.pad(k, ((0, 0), (0, pad), (0, 0), (0, 0)))
    vp = jnp.pad(v, ((0, 0), (0, pad), (0, 0), (0, 0)))
    kb = kp.reshape(B, nb, MOBA_BLOCK, H, hd).transpose(0, 3, 1, 2, 4)
    vb = vp.reshape(B, nb, MOBA_BLOCK, H, hd).transpose(0, 3, 1, 2, 4)
    kmean = jnp.mean(kb.astype(jnp.float32), axis=3)
    gate = jnp.einsum('bshd,bhnd->bhsn', q.astype(jnp.float32), kmean)
    pos = jnp.arange(S)
    past = jnp.arange(nb)[None, :] < (pos // MOBA_BLOCK)[:, None]
    gate = jnp.where(past[None, None], gate, -jnp.inf)
    ksel = min(MOBA_TOPK, nb)
    _, sel = lax.top_k(gate, ksel)
    nq = S // Q_BLOCK
    qq = q.reshape(B, nq, Q_BLOCK, H, hd).transpose(0, 1, 3, 2, 4).reshape(B * nq, H, Q_BLOCK, hd)
    ss = sel.reshape(B, H, nq, Q_BLOCK, ksel).transpose(0, 2, 1, 3, 4).reshape(B * nq, H, Q_BLOCK, ksel)
    b_idx = jnp.repeat(jnp.arange(B, dtype=jnp.int32), nq)
    qb_idx = jnp.tile(jnp.arange(nq, dtype=jnp.int32), B)
    slopes = alibi_slopes(H)
    scale = hd ** -0.5
    off = jnp.arange(MOBA_BLOCK)
    gather = jax.vmap(lambda kk, s: kk[s])

    def one_block(args):
        qblk, sblk, b, qi = args
        kh = kb[b]
        vh = vb[b]
        tpos = qi * Q_BLOCK + jnp.arange(Q_BLOCK)
        own = (qi * Q_BLOCK) // MOBA_BLOCK
        k_s = gather(kh, sblk).astype(jnp.float32)
        v_s = gather(vh, sblk).astype(jnp.float32)
        k_o = lax.dynamic_index_in_dim(kh, own, axis=1, keepdims=False).astype(jnp.float32)
        v_o = lax.dynamic_index_in_dim(vh, own, axis=1, keepdims=False).astype(jnp.float32)
        qf = qblk.astype(jnp.float32) * scale
        s_sel = jnp.einsum('hqd,hqjkd->hqjk', qf, k_s)
        kpos_sel = sblk[..., None] * MOBA_BLOCK + off
        dist_sel = (tpos[None, :, None, None] - kpos_sel).astype(jnp.float32)
        s_sel = s_sel - slopes[:, None, None, None] * dist_sel
        s_sel = jnp.where((sblk < own)[..., None], s_sel, -jnp.inf)
        s_own = jnp.einsum('hqd,hkd->hqk', qf, k_o)
        kpos_own = own * MOBA_BLOCK + off
        dist_own = (tpos[:, None] - kpos_own[None, :]).astype(jnp.float32)
        s_own = s_own - slopes[:, None, None] * dist_own[None]
        s_own = jnp.where((kpos_own[None, :] <= tpos[:, None])[None], s_own, -jnp.inf)
        logits = jnp.concatenate([s_sel.reshape(H, Q_BLOCK, ksel * MOBA_BLOCK), s_own], axis=-1)
        p = jax.nn.softmax(logits, axis=-1)
        p_sel = p[..., :ksel * MOBA_BLOCK].reshape(H, Q_BLOCK, ksel, MOBA_BLOCK)
        p_own = p[..., ksel * MOBA_BLOCK:]
        o = jnp.einsum('hqjk,hqjkd->qhd', p_sel, v_s) + jnp.einsum('hqk,hkd->qhd', p_own, v_o)
        return o.astype(q.dtype)

    out = lax.map(one_block, (qq, ss, b_idx, qb_idx))
    return out.reshape(B, S, H * hd)


def hgrn2(q, f_pre, i, g, lb, norm_g):
    B, S, _ = q.shape
    nc = S // REC_CHUNK
    f = lb + (1.0 - lb) * jax.nn.sigmoid(f_pre.astype(jnp.float32))
    logf = jnp.log(f)
    kk = 1.0 - f
    qf = jax.nn.silu(q.astype(jnp.float32))
    vf = i.astype(jnp.float32)

    def heads(t, d):
        return t.reshape(B, nc, REC_CHUNK, REC_HEADS, d).transpose(1, 0, 3, 2, 4)

    causal = jnp.tril(jnp.ones((REC_CHUNK, REC_CHUNK), dtype=bool))

    def step(state, inp):
        qc, lfc, kc, vc = inp
        cum = jnp.cumsum(lfc, axis=2)
        diff = cum[:, :, :, None, :] - cum[:, :, None, :, :]
        decay = jnp.where(causal[None, None, :, :, None], jnp.exp(jnp.minimum(diff, 0.0)), 0.0)
        scores = jnp.einsum('bhtd,bhtsd,bhsd->bhts', qc, decay, kc)
        o = jnp.einsum('bhts,bhsv->bhtv', scores, vc) + jnp.einsum('bhtd,bhdv->bhtv', qc * jnp.exp(cum), state)
        last = cum[:, :, -1:, :]
        state = jnp.exp(last[:, :, 0, :])[..., None] * state + jnp.einsum('bhsd,bhsv->bhdv', kc * jnp.exp(last - cum), vc)
        return state, o

    state0 = jnp.zeros((B, REC_HEADS, REC_DK, REC_DV), jnp.float32)
    _, o = lax.scan(step, state0, (heads(qf, REC_DK), heads(logf, REC_DK), heads(kk, REC_DK), heads(vf, REC_DV)))
    o = o.transpose(1, 0, 3, 2, 4).reshape(B, S, REC_HEADS, REC_DV)
    o = rmsnorm(o, norm_g.reshape(REC_HEADS, REC_DV))
    o = o * jax.nn.silu(g.astype(jnp.float32)).reshape(B, S, REC_HEADS, REC_DV)
    return o.reshape(B, S, REC_VW).astype(q.dtype)


def peer(x, wq, subkeys, u, v):
    B, S, D = x.shape
    T = B * S
    xt = x.reshape(T // PEER_TOKEN_BLOCK, PEER_TOKEN_BLOCK, D)

    def one(xb):
        qp = (xb @ wq).reshape(PEER_TOKEN_BLOCK, PEER_HEADS, 2, PEER_DKEY // 2)
        s = jnp.einsum('thcd,hcnd->thcn', qp.astype(jnp.float32), subkeys.astype(jnp.float32))
        v1, i1 = lax.top_k(s[:, :, 0], PEER_TOPK)
        v2, i2 = lax.top_k(s[:, :, 1], PEER_TOPK)
        cand = (v1[..., :, None] + v2[..., None, :]).reshape(PEER_TOKEN_BLOCK, PEER_HEADS, PEER_TOPK * PEER_TOPK)
        cidx = (i1[..., :, None] * PEER_NKEYS + i2[..., None, :]).reshape(PEER_TOKEN_BLOCK, PEER_HEADS, PEER_TOPK * PEER_TOPK)
        sc, p = lax.top_k(cand, PEER_TOPK)
        eidx = jnp.take_along_axis(cidx, p, axis=-1)
        gates = jax.nn.softmax(sc, axis=-1)
        ue = u[eidx]
        ve = v[eidx]
        act = jax.nn.gelu(jnp.einsum('td,thkd->thk', xb, ue).astype(jnp.float32), approximate=False)
        return jnp.einsum('thk,thkd->td', (gates * act).astype(x.dtype), ve)

    return lax.map(one, xt).reshape(B, S, D)


def setup_inputs(seed: int = 0) -> dict:
    key = jax.random.key(seed)
    ks = jax.random.split(key, 12)
    nrm = jax.random.normal
    x = nrm(ks[0], (BATCH, SEQ, D_MODEL), jnp.float32)
    norm1_g = 1.0 + 0.05 * nrm(ks[1], (DEPTH, D_MODEL), jnp.float32)
    w_in = nrm(ks[2], (DEPTH, D_MODEL, IN_COLS), jnp.float32) * D_MODEL ** -0.5
    rec_lb_logits = 0.5 * nrm(ks[3], (DEPTH + 1, REC_W), jnp.float32)
    rec_norm_g = 1.0 + 0.05 * nrm(ks[4], (DEPTH, REC_VW), jnp.float32)
    w_out = nrm(ks[5], (DEPTH, D_MIX, D_MODEL), jnp.float32) * D_MIX ** -0.5
    norm2_g = 1.0 + 0.05 * nrm(ks[6], (DEPTH, D_MODEL), jnp.float32)
    peer_wq = nrm(ks[7], (DEPTH, D_MODEL, PEER_HEADS * PEER_DKEY), jnp.float32) * D_MODEL ** -0.5
    peer_subkeys = nrm(ks[8], (DEPTH, PEER_HEADS, 2, PEER_NKEYS, PEER_DKEY // 2), jnp.float32) * (PEER_DKEY // 2) ** -0.5
    peer_u = nrm(ks[9], (DEPTH, PEER_N, D_MODEL), jnp.float32) * D_MODEL ** -0.5
    peer_v = 0.1 * nrm(ks[10], (DEPTH, PEER_N, D_MODEL), jnp.float32)
    normf_g = 1.0 + 0.05 * nrm(ks[11], (D_MODEL,), jnp.float32)
    return {'x': x, 'norm1_g': norm1_g, 'w_in': w_in, 'rec_lb_logits': rec_lb_logits,
            'rec_norm_g': rec_norm_g, 'w_out': w_out, 'norm2_g': norm2_g, 'peer_wq': peer_wq,
            'peer_subkeys': peer_subkeys, 'peer_u': peer_u, 'peer_v': peer_v, 'normf_g': normf_g}


def reference(x, norm1_g, w_in, rec_lb_logits, rec_norm_g, w_out, norm2_g, peer_wq, peer_subkeys, peer_u, peer_v, normf_g):
    B, S, _ = x.shape
    lb_all = jnp.cumsum(jax.nn.softmax(rec_lb_logits.astype(jnp.float32), axis=0), axis=0)
    h = x
    for layer in range(DEPTH):
        xn = rmsnorm(h, norm1_g[layer])
        proj = xn @ w_in[layer]
        q_a, k_a, v_a, q_r, f_r, i_r, g_r = jnp.split(proj, IN_SPLITS, axis=-1)
        attn = moba_attention(q_a.reshape(B, S, ATTN_HEADS, ATTN_HD),
                              k_a.reshape(B, S, ATTN_HEADS, ATTN_HD),
                              v_a.reshape(B, S, ATTN_HEADS, ATTN_HD))
        rec = hgrn2(q_r, f_r, i_r, g_r, lb_all[layer], rec_norm_g[layer])
        h = h + jnp.concatenate([attn, rec], axis=-1) @ w_out[layer]
        hn = rmsnorm(h, norm2_g[layer])
        h = h + peer(hn, peer_wq[layer], peer_subkeys[layer], peer_u[layer], peer_v[layer])
    return rmsnorm(h, normf_g)
```

```python
import functools

import jax
import jax.numpy as jnp
import numpy as np
from jax import lax
from jax.experimental import pallas as pl
from jax.experimental.pallas import tpu as pltpu

F32 = jnp.float32
BF16 = jnp.bfloat16

NORM_EPS = 1e-6
ATTN_HEADS = 8
ATTN_HD = 64
MOBA_BLOCK = 256
MOBA_TOPK = 3
REC_HEADS = 4
REC_D = 128
PEER_HEADS = 8
PEER_NKEYS = 128
PEER_TOPK = 16
REC_CHUNK = 128
REC_BAND = 16
LANES = 128
VMEM_LIMIT = 56 * 1024 * 1024

_NT = (((1,), (1,)), ((), ()))


def _dot(a, b):
    return jnp.dot(a, b, preferred_element_type=F32)


def _dot_nt(a, b):
    return lax.dot_general(a, b, _NT, preferred_element_type=F32)


def _sigmoid(x):
    return 1.0 / (1.0 + jnp.exp(-x))


def _rms(x, g):
    return x * lax.rsqrt(jnp.mean(x * x, axis=-1, keepdims=True) + NORM_EPS) * g


def _params(*sem):
    return pltpu.CompilerParams(dimension_semantics=sem, vmem_limit_bytes=VMEM_LIMIT)


def _in_proj_kernel(x_ref, g_ref, wq_ref, wk_ref, wvt_ref, wr_ref, wit_ref,
                    qa_ref, ka_ref, vat_ref, qr_ref, fr_ref, ir_ref, gr_ref, irt_ref):
    xn = _rms(x_ref[...], g_ref[...]).astype(BF16)
    qa_ref[...] = _dot(xn, wq_ref[...]).astype(BF16)
    ka_ref[...] = _dot(xn, wk_ref[...]).astype(BF16)
    vat_ref[...] = _dot_nt(wvt_ref[...], xn).astype(BF16)
    w = wr_ref.shape[1] // 4
    qr_ref[...] = _dot(xn, wr_ref[:, 0 * w:1 * w])
    fr_ref[...] = _dot(xn, wr_ref[:, 1 * w:2 * w])
    ir_ref[...] = _dot(xn, wr_ref[:, 2 * w:3 * w])
    gr_ref[...] = _dot(xn, wr_ref[:, 3 * w:4 * w])
    irt_ref[...] = _dot_nt(wit_ref[...], xn).astype(BF16)


def _in_proj(x2, g, w_in, tm):
    T, D = x2.shape
    aw = ATTN_HEADS * ATTN_HD
    rw = REC_HEADS * REC_D
    wb = w_in.astype(BF16)
    wq, wk, wv = wb[:, 0:aw], wb[:, aw:2 * aw], wb[:, 2 * aw:3 * aw]
    wr = wb[:, 3 * aw:]
    wi = wb[:, 3 * aw + 2 * rw:3 * aw + 3 * rw]
    tok = lambda n: pl.BlockSpec((tm, n), lambda i: (i, 0))
    tokt = lambda n: pl.BlockSpec((n, tm), lambda i: (0, i))
    full = lambda a: pl.BlockSpec(a.shape, lambda i: (0, 0))
    ops = (x2, g.reshape(1, D), wq, wk, wv.T, wr, wi.T)
    return pl.pallas_call(
        _in_proj_kernel,
        grid=(T // tm,),
        in_specs=[tok(D)] + [full(a) for a in ops[1:]],
        out_specs=[tok(aw), tok(aw), tokt(aw), tok(rw), tok(rw), tok(rw), tok(rw), tokt(rw)],
        out_shape=[jax.ShapeDtypeStruct((T, aw), BF16), jax.ShapeDtypeStruct((T, aw), BF16),
                   jax.ShapeDtypeStruct((aw, T), BF16),
                   jax.ShapeDtypeStruct((T, rw), F32), jax.ShapeDtypeStruct((T, rw), F32),
                   jax.ShapeDtypeStruct((T, rw), F32), jax.ShapeDtypeStruct((T, rw), F32),
                   jax.ShapeDtypeStruct((rw, T), BF16)],
        compiler_params=_params("parallel"),
        name="in_proj",
    )(*ops)


def _moba_kernel(slopes_ref, q_ref, k_ref, vt_ref, o_ref, s_scr, *, nb):
    blk = MOBA_BLOCK
    hp = pl.program_id(1)
    neg = -jnp.inf
    lane = lax.broadcasted_iota(jnp.int32, (1, LANES), 1)
    rel = (lax.broadcasted_iota(jnp.int32, (blk, blk), 1)
           - lax.broadcasted_iota(jnp.int32, (blk, blk), 0)).astype(F32)
    kmean = jnp.concatenate(
        [jnp.mean(k_ref[0, j * blk:(j + 1) * blk, :].astype(F32), axis=0, keepdims=True)
         for j in range(nb)], axis=0)
    heads = []
    for hh in range(2):
        in_head = (lane >= hh * ATTN_HD) & (lane < (hh + 1) * ATTN_HD)
        km = jnp.where(in_head, kmean, 0.0)
        km_hi = km.astype(BF16)
        km_lo = (km - km_hi.astype(F32)).astype(BF16)
        qmask = jnp.where(in_head, ATTN_HD ** -0.5, 0.0).astype(BF16)
        heads.append((slopes_ref[2 * hp + hh], km_hi, km_lo, qmask))
    for i in range(nb):
        q_i = q_ref[0, i * blk:(i + 1) * blk, :]
        outs = []
        for hh in range(2):
            slope, km_hi, km_lo, qmask = heads[hh]
            qs = q_i * qmask
            sel = None
            if i > MOBA_TOPK:
                gate = _dot_nt(km_hi, qs) + _dot_nt(km_lo, qs)
                rows = [gate[j:j + 1, :] for j in range(i)]
                sel = []
                for j in range(i):
                    rank = jnp.zeros((1, blk), F32)
                    for m in range(i):
                        if m == j:
                            continue
                        beats = (rows[m] >= rows[j]) if m < j else (rows[m] > rows[j])
                        rank = rank + jnp.where(beats, 1.0, 0.0)
                    sel.append(rank < MOBA_TOPK)
            mx = jnp.full((1, blk), neg, F32)
            for j in range(i + 1):
                st = _dot_nt(k_ref[0, j * blk:(j + 1) * blk, :], qs)
                st = st - slope * (rel + float((i - j) * blk))
                if j == i:
                    st = jnp.where(rel >= 0.0, st, neg)
                elif sel is not None:
                    st = jnp.where(sel[j], st, neg)
                s_scr[j] = st
                mx = jnp.maximum(mx, jnp.max(st, axis=0, keepdims=True))
            den = jnp.zeros((1, blk), F32)
            acc = jnp.zeros((ATTN_HD, blk), F32)
            for j in range(i + 1):
                p = jnp.exp(s_scr[j] - mx)
                den = den + jnp.sum(p, axis=0, keepdims=True)
                acc = acc + _dot(vt_ref[hh * ATTN_HD:(hh + 1) * ATTN_HD, j * blk:(j + 1) * blk],
                                 p.astype(BF16))
            outs.append(acc / den)
        o_ref[0, i * blk:(i + 1) * blk, :] = jnp.concatenate(outs, axis=0).T.astype(o_ref.dtype)


def _moba(qa, ka, vat, B, S):
    aw = ATTN_HEADS * ATTN_HD
    nb = S // MOBA_BLOCK
    slopes = 2.0 ** (-8.0 * jnp.arange(1, ATTN_HEADS + 1, dtype=F32) / ATTN_HEADS)
    seq = pl.BlockSpec((1, S, LANES), lambda b, h: (b, 0, h))
    return pl.pallas_call(
        functools.partial(_moba_kernel, nb=nb),
        grid=(B, ATTN_HEADS // 2),
        in_specs=[pl.BlockSpec(memory_space=pltpu.SMEM), seq, seq,
                  pl.BlockSpec((LANES, S), lambda b, h: (h, b))],
        out_specs=seq,
        out_shape=jax.ShapeDtypeStruct((B, S, aw), BF16),
        scratch_shapes=[pltpu.VMEM((nb, MOBA_BLOCK, MOBA_BLOCK), F32)],
        compiler_params=_params("parallel", "parallel"),
        name="moba",
    )(slopes, qa.reshape(B, S, aw), ka.reshape(B, S, aw), vat)


def _hgrn_kernel(q_ref, f_ref, i_ref, g_ref, it_ref, lb_ref, ng_ref, o_ref, st_scr, lvl_scr, *, nchunks):
    C = REC_CHUNK
    row = lax.broadcasted_iota(jnp.int32, (C, REC_D), 0)
    rr = lax.broadcasted_iota(jnp.int32, (C, C), 0)
    cc = lax.broadcasted_iota(jnp.int32, (C, C), 1)
    lvl = jnp.zeros((C, C), jnp.int32)
    for c in (64, 32, 16):
        sh_c = c.bit_length() - 1
        rb_c = jnp.right_shift(rr, sh_c)
        hit = ((rb_c & 1) == 1) & (jnp.right_shift(cc, sh_c) == rb_c - 1)
        lvl = jnp.where(hit, c, lvl)
    lvl_scr[...] = lvl
    st_scr[...] = jnp.zeros_like(st_scr)
    lb = lb_ref[...]
    ng = ng_ref[...]

    def chunk(ci, carry):
        r0 = pl.multiple_of(ci * C, C)
        qc = q_ref[0, pl.ds(r0, C), :]
        fc = f_ref[0, pl.ds(r0, C), :]
        vc = i_ref[0, pl.ds(r0, C), :]
        gc = g_ref[0, pl.ds(r0, C), :]
        f = lb + (1.0 - lb) * _sigmoid(fc)
        kk = 1.0 - f
        qs = qc * _sigmoid(qc)
        cum = jnp.log(f)
        sh = 1
        while sh < C:
            cum = cum + jnp.where(row >= sh, pltpu.roll(cum, sh, axis=0), 0.0)
            sh *= 2
        tot = cum[C - 1:C, :]
        vb = vc.astype(BF16)
        state_t = st_scr[...]
        out = _dot_nt((qs * jnp.exp(cum)).astype(BF16), state_t.astype(BF16))
        k_end = (kk * jnp.exp(tot - cum)).astype(BF16)
        st_scr[...] = state_t * jnp.exp(tot) + _dot(it_ref[:, pl.ds(r0, C)], k_end)
        lvl_m = lvl_scr[...]
        a_mat = jnp.zeros((C, C), F32)
        for c in (64, 32, 16):
            nblk = C // c
            ends = [cum[(b + 1) * c - 1:(b + 1) * c, :] for b in range(nblk)]
            r_q = jnp.zeros((C, REC_D), F32)
            r_k = jnp.broadcast_to(ends[0], (C, REC_D))
            for b in range(1, nblk):
                r_q = jnp.where(row >= b * c, ends[b - 1], r_q)
                r_k = jnp.where(row >= b * c, ends[b], r_k)
            qd = qs * jnp.exp(cum - r_q)
            kd = kk * jnp.exp(r_k - cum)
            a_c = _dot_nt(qd.astype(BF16), kd.astype(BF16))
            a_mat = jnp.where(lvl_m == c, a_c, a_mat)
        out = out + _dot(a_mat.astype(BF16), vb)
        pos = row & (REC_BAND - 1)
        for d in range(REC_BAND):
            if d == 0:
                y = qs * kk
                vs = vc
            else:
                ks = pltpu.roll(kk, d, axis=0)
                cs = pltpu.roll(cum, d, axis=0)
                vs = pltpu.roll(vc, d, axis=0)
                y = qs * ks * jnp.exp(jnp.minimum(cum - cs, 0.0))
                y = jnp.where(pos >= d, y, 0.0)
            out = out + jnp.sum(y, axis=-1, keepdims=True) * vs
        out = _rms(out, ng) * (gc * _sigmoid(gc))
        o_ref[0, pl.ds(r0, C), :] = out.astype(o_ref.dtype)
        return carry

    lax.fori_loop(0, nchunks, chunk, 0)


def _hgrn(qr, fr, ir, gr, irt, lb, ng, B, S):
    rw = REC_HEADS * REC_D
    seq = pl.BlockSpec((1, S, REC_D), lambda b, h: (b, 0, h))
    vec = pl.BlockSpec((1, REC_D), lambda b, h: (0, h))
    r3 = lambda a: a.reshape(B, S, rw)
    return pl.pallas_call(
        functools.partial(_hgrn_kernel, nchunks=S // REC_CHUNK),
        grid=(B, REC_HEADS),
        in_specs=[seq, seq, seq, seq, pl.BlockSpec((REC_D, S), lambda b, h: (h, b)), vec, vec],
        out_specs=seq,
        out_shape=jax.ShapeDtypeStruct((B, S, rw), BF16),
        scratch_shapes=[pltpu.VMEM((REC_D, REC_D), F32), pltpu.VMEM((REC_CHUNK, REC_CHUNK), jnp.int32)],
        compiler_params=_params("parallel", "parallel"),
        name="hgrn2",
    )(r3(qr), r3(fr), r3(ir), r3(gr), irt, lb.reshape(1, rw), ng.reshape(1, rw))


def _out_proj_kernel(x_ref, a_ref, r_ref, wa_ref, wr_ref, g_ref, wq_ref, h_ref, hn_ref, qp_ref):
    h = x_ref[...] + _dot(a_ref[...], wa_ref[...]) + _dot(r_ref[...], wr_ref[...])
    h_ref[...] = h
    hn = _rms(h, g_ref[...]).astype(BF16)
    hn_ref[...] = hn
    qp_ref[...] = _dot(hn, wq_ref[...]).astype(BF16)


def _out_proj(x2, attn, rec, w_out, g2, wq, tm):
    T, D = x2.shape
    aw = attn.shape[1]
    rw = rec.shape[1]
    wo = w_out.astype(BF16)
    wqb = wq.astype(BF16)
    nq = wqb.shape[1]
    tok = lambda n: pl.BlockSpec((tm, n), lambda i: (i, 0))
    full = lambda a: pl.BlockSpec(a.shape, lambda i: (0, 0))
    ops = (x2, attn, rec, wo[:aw], wo[aw:], g2.reshape(1, D), wqb)
    return pl.pallas_call(
        _out_proj_kernel,
        grid=(T // tm,),
        in_specs=[tok(D), tok(aw), tok(rw)] + [full(a) for a in ops[3:]],
        out_specs=[tok(D), tok(D), tok(nq)],
        out_shape=[jax.ShapeDtypeStruct((T, D), F32), jax.ShapeDtypeStruct((T, D), BF16),
                   jax.ShapeDtypeStruct((T, nq), BF16)],
        compiler_params=_params("parallel"),
        name="out_proj",
    )(*ops)


def _extract_desc(x, n):
    out = []
    prev = None
    for _ in range(n):
        cur = x if prev is None else jnp.where(x < prev, x, -jnp.inf)
        prev = jnp.max(cur, axis=0, keepdims=True)
        out.append(prev)
    return out


def _peer_topk_kernel(qp_ref, sk_ref, s1_ref, s2_ref, st_ref, v_scr):
    k = PEER_TOPK
    v_scr[...] = jnp.full(v_scr.shape, -jnp.inf, F32)
    for c in range(2):
        s_t = _dot_nt(sk_ref[0, c], qp_ref[:, c * PEER_NKEYS:(c + 1) * PEER_NKEYS])
        (s1_ref if c == 0 else s2_ref)[0] = s_t
        for r, v in enumerate(_extract_desc(s_t, k + 1)):
            v_scr[c, r:r + 1, :] = v
    v1 = v_scr[0]
    v2 = v_scr[1]
    cands = [v2 + v1[0:1]]
    cands += [v2[0:8] + v1[a:a + 1] for a in range(1, 8)]
    cands += [v1[8:] + v2[0:1]]
    taus = _extract_desc(jnp.concatenate(cands, axis=0), k + 1)
    z = jnp.zeros_like(taus[0])
    for r in range(k):
        z = z + jnp.exp(taus[r] - taus[0])
    tmid = 0.5 * (taus[k - 1] + taus[k])
    zero = jnp.zeros_like(z)
    st_ref[0] = jnp.concatenate([tmid, v1[0:1], v2[0:1], z, zero, zero, zero, zero], axis=0)


def _peer_topk(qp, subkeys, tb):
    T = qp.shape[0]
    dk = 2 * PEER_NKEYS
    skb = subkeys.astype(BF16)
    tile = pl.BlockSpec((1, PEER_NKEYS, tb), lambda i, h: (h, 0, i))
    return pl.pallas_call(
        _peer_topk_kernel,
        grid=(T // tb, PEER_HEADS),
        in_specs=[pl.BlockSpec((tb, dk), lambda i, h: (i, h)),
                  pl.BlockSpec((1, 2, PEER_NKEYS, dk // 2), lambda i, h: (h, 0, 0, 0))],
        out_specs=[tile, tile, pl.BlockSpec((1, 8, tb), lambda i, h: (h, 0, i))],
        out_shape=[jax.ShapeDtypeStruct((PEER_HEADS, PEER_NKEYS, T), F32),
                   jax.ShapeDtypeStruct((PEER_HEADS, PEER_NKEYS, T), F32),
                   jax.ShapeDtypeStruct((PEER_HEADS, 8, T), F32)],
        scratch_shapes=[pltpu.VMEM((2, PEER_TOPK + 8, tb), F32)],
        compiler_params=_params("parallel", "parallel"),
        name="peer_topk",
    )(qp, skb)


def _peer_main_kernel(hn_ref, u_ref, vt_ref, s1_ref, s2_ref, st_ref, o_ref,
                      e2_scr, th_scr, a_scr, act_scr, p_scr, *, tb, et):
    j = pl.program_id(1)
    nk = PEER_NKEYS

    @pl.when(j == 0)
    def _():
        for h in range(PEER_HEADS):
            st = st_ref[h]
            tmid, m1, m2, z = st[0:1], st[1:2], st[2:3], st[3:4]
            s1 = s1_ref[h]
            e2_scr[h] = jnp.exp(s2_ref[h] - m2)
            th_scr[h] = jnp.exp((tmid - m2) - s1)
            a_scr[h] = jnp.exp(s1 - m1) / z
        o_ref[...] = jnp.zeros_like(o_ref)

    act_scr[...] = _dot_nt(u_ref[...], hn_ref[...])
    rb = 32
    n1 = et // nk
    i1_0 = pl.multiple_of(j * n1, n1)
    for lg in range(tb // LANES):
        ls = slice(lg * LANES, (lg + 1) * LANES)
        th_all = [th_scr[h, pl.ds(i1_0, n1), ls] for h in range(PEER_HEADS)]
        a_all = [a_scr[h, pl.ds(i1_0, n1), ls] for h in range(PEER_HEADS)]
        for ii in range(n1):
            th = [t[ii:ii + 1, :] for t in th_all]
            aa = [a[ii:ii + 1, :] for a in a_all]
            for r in range(nk // rb):
                rs = slice(r * rb, (r + 1) * rb)
                es = slice(ii * nk + r * rb, ii * nk + (r + 1) * rb)
                w = jnp.zeros((rb, LANES), F32)
                for h in range(PEER_HEADS):
                    e2 = e2_scr[h, rs, ls]
                    w = w + jnp.where(e2 >= th[h], aa[h] * e2, 0.0)
                x = act_scr[es, ls]
                gelu = 0.5 * x * (1.0 + lax.erf(x * (2.0 ** -0.5)))
                p_scr[es, ls] = (w * gelu).astype(BF16)
    o_ref[...] += _dot(vt_ref[...], p_scr[...])


def _peer_main(hn, u, v, s1t, s2t, stats, tb, et):
    T, D = hn.shape
    N = u.shape[0]
    ub = u.astype(BF16)
    vtb = v.astype(BF16).T
    sc = pl.BlockSpec((PEER_HEADS, PEER_NKEYS, tb), lambda i, j: (0, 0, i))
    return pl.pallas_call(
        functools.partial(_peer_main_kernel, tb=tb, et=et),
        grid=(T // tb, N // et),
        in_specs=[pl.BlockSpec((tb, D), lambda i, j: (i, 0)),
                  pl.BlockSpec((et, D), lambda i, j: (j, 0)),
                  pl.BlockSpec((D, et), lambda i, j: (0, j)),
                  sc, sc, pl.BlockSpec((PEER_HEADS, 8, tb), lambda i, j: (0, 0, i))],
        out_specs=pl.BlockSpec((D, tb), lambda i, j: (0, i)),
        out_shape=jax.ShapeDtypeStruct((D, T), F32),
        scratch_shapes=[pltpu.VMEM((PEER_HEADS, PEER_NKEYS, tb), F32)] * 3
        + [pltpu.VMEM((et, tb), F32), pltpu.VMEM((et, tb), BF16)],
        compiler_params=_params("parallel", "arbitrary"),
        name="peer_main",
    )(hn, ub, vtb, s1t, s2t, stats)


def _final_kernel(pt_ref, h_ref, g_ref, o_ref):
    o_ref[...] = _rms(h_ref[...] + pt_ref[...].T, g_ref[...])


def _final(peer_t, h1, g, tm):
    T, D = h1.shape
    return pl.pallas_call(
        _final_kernel,
        grid=(T // tm,),
        in_specs=[pl.BlockSpec((D, tm), lambda i: (0, i)), pl.BlockSpec((tm, D), lambda i: (i, 0)),
                  pl.BlockSpec((1, D), lambda i: (0, 0))],
        out_specs=pl.BlockSpec((tm, D), lambda i: (i, 0)),
        out_shape=jax.ShapeDtypeStruct((T, D), F32),
        compiler_params=_params("parallel"),
        name="final",
    )(peer_t, h1, g.reshape(1, D))


def _residual_kernel(pt_ref, h_ref, o_ref):
    o_ref[...] = h_ref[...] + pt_ref[...].T


def _residual(peer_t, h1, tm):
    T, D = h1.shape
    return pl.pallas_call(
        _residual_kernel,
        grid=(T // tm,),
        in_specs=[pl.BlockSpec((D, tm), lambda i: (0, i)), pl.BlockSpec((tm, D), lambda i: (i, 0))],
        out_specs=pl.BlockSpec((tm, D), lambda i: (i, 0)),
        out_shape=jax.ShapeDtypeStruct((T, D), F32),
        compiler_params=_params("parallel"),
        name="residual",
    )(peer_t, h1)


def _tile(n, want):
    t = min(n, want)
    assert n % t == 0, (n, t)
    return t


def kernel(x, norm1_g, w_in, rec_lb_logits, rec_norm_g, w_out, norm2_g, peer_wq, peer_subkeys, peer_u,
           peer_v, normf_g):
    B, S, D = x.shape
    T = B * S
    depth = norm1_g.shape[0]
    assert S % MOBA_BLOCK == 0 and S % REC_CHUNK == 0
    tm = _tile(T, 512)
    tb_topk = _tile(T, 512)
    tb_peer = _tile(T, 512)
    et = 8 * PEER_NKEYS
    lb_all = jnp.cumsum(jax.nn.softmax(rec_lb_logits.astype(F32), axis=0), axis=0)
    h = x.reshape(T, D)
    out = None
    for layer in range(depth):
        qa, ka, vat, qr, fr, ir, gr, irt = _in_proj(h, norm1_g[layer], w_in[layer], tm)
        attn = _moba(qa, ka, vat, B, S).reshape(T, -1)
        rec = _hgrn(qr, fr, ir, gr, irt, lb_all[layer], rec_norm_g[layer], B, S).reshape(T, -1)
        h1, hn, qp = _out_proj(h, attn, rec, w_out[layer], norm2_g[layer], peer_wq[layer], tm)
        s1t, s2t, stats = _peer_topk(qp, peer_subkeys[layer], tb_topk)
        peer_t = _peer_main(hn, peer_u[layer], peer_v[layer], s1t, s2t, stats, tb_peer, et)
        if layer == depth - 1:
            out = _final(peer_t, h1, normf_g, tm)
        else:
            h = _residual(peer_t, h1, tm)
    return out.reshape(B, S, D)
```

```python
import functools

import jax
import jax.numpy as jnp
import numpy as np
from jax import lax
from jax.experimental import pallas as pl
from jax.experimental.pallas import tpu as pltpu

F32 = jnp.float32
BF16 = jnp.bfloat16

NORM_EPS = 1e-6
ATTN_HEADS = 8
ATTN_HD = 64
MOBA_BLOCK = 256
MOBA_TOPK = 3
REC_HEADS = 4
REC_D = 128
PEER_HEADS = 8
PEER_NKEYS = 128
PEER_TOPK = 16
REC_CHUNK = 128
REC_BAND = 16
LANES = 128
GELU_SCALE = 2.0 ** -0.5
VMEM_LIMIT = 56 * 1024 * 1024

_NT = (((1,), (1,)), ((), ()))


def _dot(a, b):
    return jnp.dot(a, b, preferred_element_type=F32)


def _dot_nt(a, b):
    return lax.dot_general(a, b, _NT, preferred_element_type=F32)


def _sigmoid(x):
    return 1.0 / (1.0 + jnp.exp(-x))


def _rms(x, g):
    return x * lax.rsqrt(jnp.mean(x * x, axis=-1, keepdims=True) + NORM_EPS) * g


def _params(*sem):
    return pltpu.CompilerParams(dimension_semantics=sem, vmem_limit_bytes=VMEM_LIMIT)


def _in_proj_kernel(x_ref, g_ref, wq_ref, wk_ref, wvt_ref, wr_ref, wit_ref,
                    qa_ref, ka_ref, vat_ref, qr_ref, fr_ref, ir_ref, gr_ref, irt_ref):
    xn = _rms(x_ref[...], g_ref[...]).astype(BF16)
    qa_ref[...] = _dot(xn, wq_ref[...]).astype(BF16)
    ka_ref[...] = _dot(xn, wk_ref[...]).astype(BF16)
    vat_ref[...] = _dot_nt(wvt_ref[...], xn).astype(BF16)
    w = wr_ref.shape[1] // 4
    qr_ref[...] = _dot(xn, wr_ref[:, 0 * w:1 * w])
    fr_ref[...] = _dot(xn, wr_ref[:, 1 * w:2 * w])
    ir_ref[...] = _dot(xn, wr_ref[:, 2 * w:3 * w])
    gr_ref[...] = _dot(xn, wr_ref[:, 3 * w:4 * w])
    irt_ref[...] = _dot_nt(wit_ref[...], xn).astype(BF16)


def _in_proj(x2, g, w_in, tm):
    T, D = x2.shape
    aw = ATTN_HEADS * ATTN_HD
    rw = REC_HEADS * REC_D
    wb = w_in.astype(BF16)
    wq, wk, wv = wb[:, 0:aw], wb[:, aw:2 * aw], wb[:, 2 * aw:3 * aw]
    wr = wb[:, 3 * aw:]
    wi = wb[:, 3 * aw + 2 * rw:3 * aw + 3 * rw]
    tok = lambda n: pl.BlockSpec((tm, n), lambda i: (i, 0))
    tokt = lambda n: pl.BlockSpec((n, tm), lambda i: (0, i))
    full = lambda a: pl.BlockSpec(a.shape, lambda i: (0, 0))
    ops = (x2, g.reshape(1, D), wq, wk, wv.T, wr, wi.T)
    return pl.pallas_call(
        _in_proj_kernel,
        grid=(T // tm,),
        in_specs=[tok(D)] + [full(a) for a in ops[1:]],
        out_specs=[tok(aw), tok(aw), tokt(aw), tok(rw), tok(rw), tok(rw), tok(rw), tokt(rw)],
        out_shape=[jax.ShapeDtypeStruct((T, aw), BF16), jax.ShapeDtypeStruct((T, aw), BF16),
                   jax.ShapeDtypeStruct((aw, T), BF16),
                   jax.ShapeDtypeStruct((T, rw), F32), jax.ShapeDtypeStruct((T, rw), F32),
                   jax.ShapeDtypeStruct((T, rw), F32), jax.ShapeDtypeStruct((T, rw), F32),
                   jax.ShapeDtypeStruct((rw, T), BF16)],
        compiler_params=_params("parallel"),
        name="in_proj",
    )(*ops)


def _moba_kernel(slopes_ref, q_ref, k_ref, vt_ref, o_ref, s_scr, *, nb):
    blk = MOBA_BLOCK
    hp = pl.program_id(1)
    neg = -jnp.inf
    lane = lax.broadcasted_iota(jnp.int32, (1, LANES), 1)
    rel = (lax.broadcasted_iota(jnp.int32, (blk, blk), 1)
           - lax.broadcasted_iota(jnp.int32, (blk, blk), 0)).astype(F32)
    kmean = jnp.concatenate(
        [jnp.mean(k_ref[0, j * blk:(j + 1) * blk, :].astype(F32), axis=0, keepdims=True)
         for j in range(nb)], axis=0)
    heads = []
    for hh in range(2):
        in_head = (lane >= hh * ATTN_HD) & (lane < (hh + 1) * ATTN_HD)
        km = jnp.where(in_head, kmean, 0.0)
        km_hi = km.astype(BF16)
        km_lo = (km - km_hi.astype(F32)).astype(BF16)
        qmask = jnp.where(in_head, ATTN_HD ** -0.5, 0.0).astype(BF16)
        heads.append((slopes_ref[2 * hp + hh], km_hi, km_lo, qmask))
    for i in range(nb):
        q_i = q_ref[0, i * blk:(i + 1) * blk, :]
        outs = []
        for hh in range(2):
            slope, km_hi, km_lo, qmask = heads[hh]
            qs = q_i * qmask
            sel = None
            if i > MOBA_TOPK:
                gate = _dot_nt(km_hi, qs) + _dot_nt(km_lo, qs)
                rows = [gate[j:j + 1, :] for j in range(i)]
                sel = []
                for j in range(i):
                    rank = jnp.zeros((1, blk), F32)
                    for m in range(i):
                        if m == j:
                            continue
                        beats = (rows[m] >= rows[j]) if m < j else (rows[m] > rows[j])
                        rank = rank + jnp.where(beats, 1.0, 0.0)
                    sel.append(rank < MOBA_TOPK)
            mx = jnp.full((1, blk), neg, F32)
            for j in range(i + 1):
                st = _dot_nt(k_ref[0, j * blk:(j + 1) * blk, :], qs)
                st = st - slope * (rel + float((i - j) * blk))
                if j == i:
                    st = jnp.where(rel >= 0.0, st, neg)
                elif sel is not None:
                    st = jnp.where(sel[j], st, neg)
                s_scr[j] = st
                mx = jnp.maximum(mx, jnp.max(st, axis=0, keepdims=True))
            den = jnp.zeros((1, blk), F32)
            acc = jnp.zeros((ATTN_HD, blk), F32)
            for j in range(i + 1):
                p = jnp.exp(s_scr[j] - mx)
                den = den + jnp.sum(p, axis=0, keepdims=True)
                acc = acc + _dot(vt_ref[hh * ATTN_HD:(hh + 1) * ATTN_HD, j * blk:(j + 1) * blk],
                                 p.astype(BF16))
            outs.append(acc / den)
        o_ref[0, i * blk:(i + 1) * blk, :] = jnp.concatenate(outs, axis=0).T.astype(o_ref.dtype)


def _moba(qa, ka, vat, B, S):
    aw = ATTN_HEADS * ATTN_HD
    nb = S // MOBA_BLOCK
    slopes = 2.0 ** (-8.0 * jnp.arange(1, ATTN_HEADS + 1, dtype=F32) / ATTN_HEADS)
    seq = pl.BlockSpec((1, S, LANES), lambda b, h: (b, 0, h))
    return pl.pallas_call(
        functools.partial(_moba_kernel, nb=nb),
        grid=(B, ATTN_HEADS // 2),
        in_specs=[pl.BlockSpec(memory_space=pltpu.SMEM), seq, seq,
                  pl.BlockSpec((LANES, S), lambda b, h: (h, b))],
        out_specs=seq,
        out_shape=jax.ShapeDtypeStruct((B, S, aw), BF16),
        scratch_shapes=[pltpu.VMEM((nb, MOBA_BLOCK, MOBA_BLOCK), F32)],
        compiler_params=_params("parallel", "parallel"),
        name="moba",
    )(slopes, qa.reshape(B, S, aw), ka.reshape(B, S, aw), vat)


def _hgrn_kernel(q_ref, f_ref, i_ref, g_ref, it_ref, lb_ref, ng_ref, o_ref, st_scr, lvl_scr, *, nchunks):
    C = REC_CHUNK
    row = lax.broadcasted_iota(jnp.int32, (C, REC_D), 0)
    rr = lax.broadcasted_iota(jnp.int32, (C, C), 0)
    cc = lax.broadcasted_iota(jnp.int32, (C, C), 1)
    lvl = jnp.zeros((C, C), jnp.int32)
    for c in (64, 32, 16):
        sh_c = c.bit_length() - 1
        rb_c = jnp.right_shift(rr, sh_c)
        hit = ((rb_c & 1) == 1) & (jnp.right_shift(cc, sh_c) == rb_c - 1)
        lvl = jnp.where(hit, c, lvl)
    lvl_scr[...] = lvl
    st_scr[...] = jnp.zeros_like(st_scr)
    lb = lb_ref[...]
    ng = ng_ref[...]

    def chunk(ci, carry):
        r0 = pl.multiple_of(ci * C, C)
        qc = q_ref[0, pl.ds(r0, C), :]
        fc = f_ref[0, pl.ds(r0, C), :]
        vc = i_ref[0, pl.ds(r0, C), :]
        gc = g_ref[0, pl.ds(r0, C), :]
        f = lb + (1.0 - lb) * _sigmoid(fc)
        kk = 1.0 - f
        qs = qc * _sigmoid(qc)
        cum = jnp.log(f)
        sh = 1
        while sh < C:
            cum = cum + jnp.where(row >= sh, pltpu.roll(cum, sh, axis=0), 0.0)
            sh *= 2
        tot = cum[C - 1:C, :]
        vb = vc.astype(BF16)
        state_t = st_scr[...]
        out = _dot_nt((qs * jnp.exp(cum)).astype(BF16), state_t.astype(BF16))
        k_end = (kk * jnp.exp(tot - cum)).astype(BF16)
        st_scr[...] = state_t * jnp.exp(tot) + _dot(it_ref[:, pl.ds(r0, C)], k_end)
        lvl_m = lvl_scr[...]
        a_mat = jnp.zeros((C, C), F32)
        for c in (64, 32, 16):
            nblk = C // c
            ends = [cum[(b + 1) * c - 1:(b + 1) * c, :] for b in range(nblk)]
            r_q = jnp.zeros((C, REC_D), F32)
            r_k = jnp.broadcast_to(ends[0], (C, REC_D))
            for b in range(1, nblk):
                r_q = jnp.where(row >= b * c, ends[b - 1], r_q)
                r_k = jnp.where(row >= b * c, ends[b], r_k)
            qd = qs * jnp.exp(cum - r_q)
            kd = kk * jnp.exp(r_k - cum)
            a_c = _dot_nt(qd.astype(BF16), kd.astype(BF16))
            a_mat = jnp.where(lvl_m == c, a_c, a_mat)
        out = out + _dot(a_mat.astype(BF16), vb)
        pos = row & (REC_BAND - 1)
        for d in range(REC_BAND):
            if d == 0:
                y = qs * kk
                vs = vc
            else:
                ks = pltpu.roll(kk, d, axis=0)
                cs = pltpu.roll(cum, d, axis=0)
                vs = pltpu.roll(vc, d, axis=0)
                y = qs * ks * jnp.exp(jnp.minimum(cum - cs, 0.0))
                y = jnp.where(pos >= d, y, 0.0)
            out = out + jnp.sum(y, axis=-1, keepdims=True) * vs
        out = _rms(out, ng) * (gc * _sigmoid(gc))
        o_ref[0, pl.ds(r0, C), :] = out.astype(o_ref.dtype)
        return carry

    lax.fori_loop(0, nchunks, chunk, 0)


def _hgrn(qr, fr, ir, gr, irt, lb, ng, B, S):
    rw = REC_HEADS * REC_D
    seq = pl.BlockSpec((1, S, REC_D), lambda b, h: (b, 0, h))
    vec = pl.BlockSpec((1, REC_D), lambda b, h: (0, h))
    r3 = lambda a: a.reshape(B, S, rw)
    return pl.pallas_call(
        functools.partial(_hgrn_kernel, nchunks=S // REC_CHUNK),
        grid=(B, REC_HEADS),
        in_specs=[seq, seq, seq, seq, pl.BlockSpec((REC_D, S), lambda b, h: (h, b)), vec, vec],
        out_specs=seq,
        out_shape=jax.ShapeDtypeStruct((B, S, rw), BF16),
        scratch_shapes=[pltpu.VMEM((REC_D, REC_D), F32), pltpu.VMEM((REC_CHUNK, REC_CHUNK), jnp.int32)],
        compiler_params=_params("parallel", "parallel"),
        name="hgrn2",
    )(r3(qr), r3(fr), r3(ir), r3(gr), irt, lb.reshape(1, rw), ng.reshape(1, rw))


def _out_proj_kernel(x_ref, a_ref, r_ref, wa_ref, wr_ref, g_ref, wq_ref, h_ref, hnt_ref, qp_ref):
    h = x_ref[...] + _dot(a_ref[...], wa_ref[...]) + _dot(r_ref[...], wr_ref[...])
    h_ref[...] = h
    hn = _rms(h, g_ref[...])
    hnt_ref[...] = hn.T.astype(BF16)
    qp_ref[...] = _dot(hn.astype(BF16), wq_ref[...]).astype(BF16)


def _out_proj(x2, attn, rec, w_out, g2, wq, tm):
    T, D = x2.shape
    aw = attn.shape[1]
    rw = rec.shape[1]
    wo = w_out.astype(BF16)
    wqb = wq.astype(BF16)
    nq = wqb.shape[1]
    tok = lambda n: pl.BlockSpec((tm, n), lambda i: (i, 0))
    full = lambda a: pl.BlockSpec(a.shape, lambda i: (0, 0))
    ops = (x2, attn, rec, wo[:aw], wo[aw:], g2.reshape(1, D), wqb)
    return pl.pallas_call(
        _out_proj_kernel,
        grid=(T // tm,),
        in_specs=[tok(D), tok(aw), tok(rw)] + [full(a) for a in ops[3:]],
        out_specs=[tok(D), pl.BlockSpec((D, tm), lambda i: (0, i)), tok(nq)],
        out_shape=[jax.ShapeDtypeStruct((T, D), F32), jax.ShapeDtypeStruct((D, T), BF16),
                   jax.ShapeDtypeStruct((T, nq), BF16)],
        compiler_params=_params("parallel"),
        name="out_proj",
    )(*ops)


def _extract_desc(x, n):
    out = []
    prev = None
    for _ in range(n):
        cur = x if prev is None else jnp.where(x < prev, x, -jnp.inf)
        prev = jnp.max(cur, axis=0, keepdims=True)
        out.append(prev)
    return out


def _peer_topk_kernel(qp_ref, sk_ref, a_ref, e2_ref, phi_ref, v_scr):
    k = PEER_TOPK
    v_scr[...] = jnp.full(v_scr.shape, -jnp.inf, F32)
    s_both = []
    for c in range(2):
        s_t = _dot_nt(sk_ref[0, c], qp_ref[:, c * PEER_NKEYS:(c + 1) * PEER_NKEYS])
        s_both.append(s_t)
        for r, v in enumerate(_extract_desc(s_t, k + 1)):
            v_scr[c, r:r + 1, :] = v
    v1 = v_scr[0]
    v2 = v_scr[1]
    cands = [v2 + v1[0:1]]
    cands += [v2[0:8] + v1[a:a + 1] for a in range(1, 8)]
    cands += [v1[8:] + v2[0:1]]
    taus = _extract_desc(jnp.concatenate(cands, axis=0), k + 1)
    z = jnp.zeros_like(taus[0])
    for r in range(k):
        z = z + jnp.exp(taus[r] - taus[0])
    tmid = 0.5 * (taus[k - 1] + taus[k])
    m1, m2 = v1[0:1], v2[0:1]
    scale = GELU_SCALE / z
    a_ref[0] = jnp.exp(s_both[0] - m1) * scale
    e2_ref[0] = jnp.exp(s_both[1] - m2).astype(e2_ref.dtype)
    phi_ref[0] = jnp.broadcast_to(jnp.exp(tmid - (m1 + m2)) * scale, phi_ref.shape[1:])


def _peer_topk(qp, subkeys, tb):
    T = qp.shape[0]
    dk = 2 * PEER_NKEYS
    skb = subkeys.astype(BF16)
    tile = pl.BlockSpec((1, PEER_NKEYS, tb), lambda i, h: (h, 0, i))
    return pl.pallas_call(
        _peer_topk_kernel,
        grid=(T // tb, PEER_HEADS),
        in_specs=[pl.BlockSpec((tb, dk), lambda i, h: (i, h)),
                  pl.BlockSpec((1, 2, PEER_NKEYS, dk // 2), lambda i, h: (h, 0, 0, 0))],
        out_specs=[tile, tile, pl.BlockSpec((1, 8, tb), lambda i, h: (h, 0, i))],
        out_shape=[jax.ShapeDtypeStruct((PEER_HEADS, PEER_NKEYS, T), F32),
                   jax.ShapeDtypeStruct((PEER_HEADS, PEER_NKEYS, T), BF16),
                   jax.ShapeDtypeStruct((PEER_HEADS, 8, T), F32)],
        scratch_shapes=[pltpu.VMEM((2, PEER_TOPK + 8, tb), F32)],
        compiler_params=_params("parallel", "parallel"),
        name="peer_topk",
    )(qp, skb)


PEER_ACT_CHUNKS = 4


def _peer_gate_chunk(j, s, c, a_ref, e2_ref, phi_ref, act_scr, p_scr, *, tb, et):
    nk = PEER_NKEYS
    rb = 32
    n1 = et // nk
    per = n1 // PEER_ACT_CHUNKS
    i1_0 = pl.multiple_of((2 * j + s) * n1, n1)
    rows = per * nk
    act_v = act_scr.at[pl.ds(pl.multiple_of(c * rows, rows), rows), :]
    p_v = p_scr.at[pl.ds(pl.multiple_of(c * rows, rows), rows), :]
    for lg in range(tb // LANES):
        ls = slice(lg * LANES, (lg + 1) * LANES)
        phi = [phi_ref[h, 0:1, ls].astype(BF16) for h in range(PEER_HEADS)]
        a_all = [a_ref[h, pl.ds(i1_0, n1), ls] for h in range(PEER_HEADS)]
        for q in range(per):
            shift = (n1 - (c * per + q)) % n1
            aa = [pltpu.roll(a, shift, axis=0)[0:1, :].astype(BF16) for a in a_all]
            for r in range(nk // rb):
                rs = slice(r * rb, (r + 1) * rb)
                es = slice(q * nk + r * rb, q * nk + (r + 1) * rb)
                w = jnp.zeros((rb, LANES), BF16)
                for h in range(PEER_HEADS):
                    y = aa[h] * e2_ref[h, rs, ls]
                    w = w + jnp.where(y >= phi[h], y, jnp.zeros_like(y))
                z = act_v[es, ls]
                p_v[es, ls] = w * (z * (1.0 + lax.erf(z))).astype(BF16)


def _peer_main_kernel(hnt_ref, u_ref, un_ref, vt_ref, a_ref, e2_ref, phi_ref, o_ref, acta_scr, actb_scr, pa_scr,
                      pb_scr, vk_scr, tmp_scr, *, tb, et):
    j = pl.program_id(1)
    nc = PEER_ACT_CHUNKS
    cr = et // nc
    dr = o_ref.shape[0] // nc

    @pl.when(j == 0)
    def _():
        o_ref[...] = jnp.zeros_like(o_ref)
        tmp_scr[...] = jnp.zeros_like(tmp_scr)
        pb_scr[...] = jnp.zeros_like(pb_scr)
        vk_scr[...] = jnp.zeros_like(vk_scr)

    act_bufs = (acta_scr, actb_scr)

    def act(s, c, src=None):
        rows = pl.ds(pl.multiple_of(c * cr, cr), cr)
        lhs = u_ref.at[s * et:(s + 1) * et, :][rows, :] if src is None else src[rows, :]
        act_bufs[s][rows, :] = _dot(lhs, hnt_ref[...])

    def gate(s, c):
        _peer_gate_chunk(j, s, c, a_ref, e2_ref, phi_ref, act_bufs[s], pa_scr if s == 0 else pb_scr, tb=tb, et=et)

    def value(s, c):
        rows = pl.ds(pl.multiple_of(c * dr, dr), dr)
        o_ref[rows, :] += tmp_scr[1 - s, rows, :]
        if s == 0:
            tmp_scr[0, rows, :] = _dot(vk_scr[rows, :], pb_scr[...])
        else:
            tmp_scr[1, rows, :] = _dot(vt_ref[rows, 0:et], pa_scr[...])

    half = nc // 2

    @pl.when(j == 0)
    def _():
        for c in range(half):
            act(0, c)

    def trip(s, s_act, d_act, src=None):
        def body(c, carry):
            gate(s, c)
            act(s_act, c + d_act, src)
            value(s, c)
            return carry
        return body

    lax.fori_loop(0, half, trip(0, 0, half), 0)
    lax.fori_loop(half, nc, trip(0, 1, -half), 0)
    lax.fori_loop(0, half, trip(1, 1, half), 0)
    lax.fori_loop(half, nc, trip(1, 0, -half, un_ref), 0)
    vk_scr[...] = vt_ref[:, et:2 * et]

    @pl.when(j == pl.num_programs(1) - 1)
    def _():
        o_ref[...] += tmp_scr[1] + _dot(vk_scr[...], pb_scr[...])


def _peer_main(hnt, u, v, a_t, e2_t, phi, tb, et):
    D, T = hnt.shape
    N = u.shape[0]
    ub = (u * GELU_SCALE).astype(BF16)
    vtb = v.astype(BF16).T
    un = et // 2
    sc = pl.BlockSpec((PEER_HEADS, PEER_NKEYS, tb), lambda i, j: (0, 0, i))
    return pl.pallas_call(
        functools.partial(_peer_main_kernel, tb=tb, et=et),
        grid=(T // tb, N // (2 * et)),
        in_specs=[pl.BlockSpec((D, tb), lambda i, j: (0, i)),
                  pl.BlockSpec((2 * et, D), lambda i, j: (j, 0)),
                  pl.BlockSpec((un, D), lambda i, j: (jnp.minimum((j + 1) * (2 * et // un), N // un - 1), 0)),
                  pl.BlockSpec((D, 2 * et), lambda i, j: (0, j)),
                  sc, sc, pl.BlockSpec((PEER_HEADS, 8, tb), lambda i, j: (0, 0, i))],
        out_specs=pl.BlockSpec((D, tb), lambda i, j: (0, i)),
        out_shape=jax.ShapeDtypeStruct((D, T), F32),
        scratch_shapes=[pltpu.VMEM((et, tb), F32), pltpu.VMEM((et, tb), F32),
                        pltpu.VMEM((et, tb), BF16), pltpu.VMEM((et, tb), BF16),
                        pltpu.VMEM((D, et), BF16), pltpu.VMEM((2, D, tb), F32)],
        compiler_params=_params("parallel", "arbitrary"),
        name="peer_main",
    )(hnt, ub, ub, vtb, a_t, e2_t, phi)


def _final_kernel(pt_ref, h_ref, g_ref, o_ref):
    o_ref[...] = _rms(h_ref[...] + pt_ref[...].T, g_ref[...])


def _final(peer_t, h1, g, tm):
    T, D = h1.shape
    return pl.pallas_call(
        _final_kernel,
        grid=(T // tm,),
        in_specs=[pl.BlockSpec((D, tm), lambda i: (0, i)), pl.BlockSpec((tm, D), lambda i: (i, 0)),
                  pl.BlockSpec((1, D), lambda i: (0, 0))],
        out_specs=pl.BlockSpec((tm, D), lambda i: (i, 0)),
        out_shape=jax.ShapeDtypeStruct((T, D), F32),
        compiler_params=_params("parallel"),
        name="final",
    )(peer_t, h1, g.reshape(1, D))


def _residual_kernel(pt_ref, h_ref, o_ref):
    o_ref[...] = h_ref[...] + pt_ref[...].T


def _residual(peer_t, h1, tm):
    T, D = h1.shape
    return pl.pallas_call(
        _residual_kernel,
        grid=(T // tm,),
        in_specs=[pl.BlockSpec((D, tm), lambda i: (0, i)), pl.BlockSpec((tm, D), lambda i: (i, 0))],
        out_specs=pl.BlockSpec((tm, D), lambda i: (i, 0)),
        out_shape=jax.ShapeDtypeStruct((T, D), F32),
        compiler_params=_params("parallel"),
        name="residual",
    )(peer_t, h1)


def _tile(n, want):
    t = min(n, want)
    assert n % t == 0, (n, t)
    return t


def kernel(x, norm1_g, w_in, rec_lb_logits, rec_norm_g, w_out, norm2_g, peer_wq, peer_subkeys, peer_u,
           peer_v, normf_g):
    B, S, D = x.shape
    T = B * S
    depth = norm1_g.shape[0]
    assert S % MOBA_BLOCK == 0 and S % REC_CHUNK == 0
    tm = _tile(T, 512)
    tb_topk = _tile(T, 512)
    tb_peer = _tile(T, 512)
    et = 8 * PEER_NKEYS
    lb_all = jnp.cumsum(jax.nn.softmax(rec_lb_logits.astype(F32), axis=0), axis=0)
    h = x.reshape(T, D)
    out = None
    for layer in range(depth):
        qa, ka, vat, qr, fr, ir, gr, irt = _in_proj(h, norm1_g[layer], w_in[layer], tm)
        attn = _moba(qa, ka, vat, B, S).reshape(T, -1)
        rec = _hgrn(qr, fr, ir, gr, irt, lb_all[layer], rec_norm_g[layer], B, S).reshape(T, -1)
        h1, hnt, qp = _out_proj(h, attn, rec, w_out[layer], norm2_g[layer], peer_wq[layer], tm)
        a_t, e2_t, phi = _peer_topk(qp, peer_subkeys[layer], tb_topk)
        peer_t = _peer_main(hnt, peer_u[layer], peer_v[layer], a_t, e2_t, phi, tb_peer, et)
        if layer == depth - 1:
            out = _final(peer_t, h1, normf_g, tm)
        else:
            h = _residual(peer_t, h1, tm)
    return out.reshape(B, S, D)
```

```python
import functools

import jax
import jax.numpy as jnp
import numpy as np
from jax import lax
from jax.experimental import pallas as pl
from jax.experimental.pallas import tpu as pltpu

F32 = jnp.float32
BF16 = jnp.bfloat16

NORM_EPS = 1e-6
ATTN_HEADS = 8
ATTN_HD = 64
MOBA_BLOCK = 256
MOBA_TOPK = 3
REC_HEADS = 4
REC_D = 128
PEER_HEADS = 8
PEER_NKEYS = 128
PEER_TOPK = 16
REC_CHUNK = 128
REC_BAND = 8
REC_LEVELS = tuple(c for c in (64, 32, 16, 8) if c >= REC_BAND)
LANES = 128
GELU_SCALE = 2.0 ** -0.5
VMEM_LIMIT = 56 * 1024 * 1024

_NT = (((1,), (1,)), ((), ()))


def _dot(a, b):
    return jnp.dot(a, b, preferred_element_type=F32)


def _dot_nt(a, b):
    return lax.dot_general(a, b, _NT, preferred_element_type=F32)


def _sigmoid(x):
    return 1.0 / (1.0 + jnp.exp(-x))


def _rms(x, g):
    return x * lax.rsqrt(jnp.mean(x * x, axis=-1, keepdims=True) + NORM_EPS) * g


def _params(*sem):
    return pltpu.CompilerParams(dimension_semantics=sem, vmem_limit_bytes=VMEM_LIMIT)


def _in_proj_kernel(x_ref, g_ref, wq_ref, wk_ref, wvt_ref, wr_ref, wit_ref,
                    qa_ref, ka_ref, vat_ref, qr_ref, fr_ref, ir_ref, gr_ref, irt_ref):
    xn = _rms(x_ref[...], g_ref[...]).astype(BF16)
    qa_ref[...] = _dot(xn, wq_ref[...]).astype(BF16)
    ka_ref[...] = _dot(xn, wk_ref[...]).astype(BF16)
    vat_ref[...] = _dot_nt(wvt_ref[...], xn).astype(BF16)
    w = wr_ref.shape[1] // 4
    qr_ref[...] = _dot(xn, wr_ref[:, 0 * w:1 * w])
    fr_ref[...] = _dot(xn, wr_ref[:, 1 * w:2 * w])
    ir_ref[...] = _dot(xn, wr_ref[:, 2 * w:3 * w])
    gr_ref[...] = _dot(xn, wr_ref[:, 3 * w:4 * w])
    irt_ref[...] = _dot_nt(wit_ref[...], xn).astype(BF16)


def _in_proj(x2, g, w_in, tm):
    T, D = x2.shape
    aw = ATTN_HEADS * ATTN_HD
    rw = REC_HEADS * REC_D
    wb = w_in.astype(BF16)
    wq, wk, wv = wb[:, 0:aw], wb[:, aw:2 * aw], wb[:, 2 * aw:3 * aw]
    wr = wb[:, 3 * aw:]
    wi = wb[:, 3 * aw + 2 * rw:3 * aw + 3 * rw]
    tok = lambda n: pl.BlockSpec((tm, n), lambda i: (i, 0))
    tokt = lambda n: pl.BlockSpec((n, tm), lambda i: (0, i))
    full = lambda a: pl.BlockSpec(a.shape, lambda i: (0, 0))
    ops = (x2, g.reshape(1, D), wq, wk, wv.T, wr, wi.T)
    return pl.pallas_call(
        _in_proj_kernel,
        grid=(T // tm,),
        in_specs=[tok(D)] + [full(a) for a in ops[1:]],
        out_specs=[tok(aw), tok(aw), tokt(aw), tok(rw), tok(rw), tok(rw), tok(rw), tokt(rw)],
        out_shape=[jax.ShapeDtypeStruct((T, aw), BF16), jax.ShapeDtypeStruct((T, aw), BF16),
                   jax.ShapeDtypeStruct((aw, T), BF16),
                   jax.ShapeDtypeStruct((T, rw), F32), jax.ShapeDtypeStruct((T, rw), F32),
                   jax.ShapeDtypeStruct((T, rw), F32), jax.ShapeDtypeStruct((T, rw), F32),
                   jax.ShapeDtypeStruct((rw, T), BF16)],
        compiler_params=_params("parallel"),
        name="in_proj",
    )(*ops)


def _moba_kernel(slopes_ref, q_ref, k_ref, vt_ref, o_ref, s_scr, *, nb):
    blk = MOBA_BLOCK
    hp = pl.program_id(1)
    neg = -jnp.inf
    lane = lax.broadcasted_iota(jnp.int32, (1, LANES), 1)
    rel = (lax.broadcasted_iota(jnp.int32, (blk, blk), 1)
           - lax.broadcasted_iota(jnp.int32, (blk, blk), 0)).astype(F32)
    kmean = jnp.concatenate(
        [jnp.mean(k_ref[0, j * blk:(j + 1) * blk, :].astype(F32), axis=0, keepdims=True)
         for j in range(nb)], axis=0)
    heads = []
    for hh in range(2):
        in_head = (lane >= hh * ATTN_HD) & (lane < (hh + 1) * ATTN_HD)
        km = jnp.where(in_head, kmean, 0.0)
        km_hi = km.astype(BF16)
        km_lo = (km - km_hi.astype(F32)).astype(BF16)
        qmask = jnp.where(in_head, ATTN_HD ** -0.5, 0.0).astype(BF16)
        heads.append((slopes_ref[2 * hp + hh], km_hi, km_lo, qmask))
    for i in range(nb):
        q_i = q_ref[0, i * blk:(i + 1) * blk, :]
        outs = []
        for hh in range(2):
            slope, km_hi, km_lo, qmask = heads[hh]
            qs = q_i * qmask
            sel = None
            if i > MOBA_TOPK:
                gate = _dot_nt(km_hi, qs) + _dot_nt(km_lo, qs)
                rows = [gate[j:j + 1, :] for j in range(i)]
                sel = []
                for j in range(i):
                    rank = jnp.zeros((1, blk), F32)
                    for m in range(i):
                        if m == j:
                            continue
                        beats = (rows[m] >= rows[j]) if m < j else (rows[m] > rows[j])
                        rank = rank + jnp.where(beats, 1.0, 0.0)
                    sel.append(rank < MOBA_TOPK)
            mx = jnp.full((1, blk), neg, F32)
            for j in range(i + 1):
                st = _dot_nt(k_ref[0, j * blk:(j + 1) * blk, :], qs)
                st = st - slope * (rel + float((i - j) * blk))
                if j == i:
                    st = jnp.where(rel >= 0.0, st, neg)
                elif sel is not None:
                    st = jnp.where(sel[j], st, neg)
                s_scr[j] = st
                mx = jnp.maximum(mx, jnp.max(st, axis=0, keepdims=True))
            den = jnp.zeros((1, blk), F32)
            acc = jnp.zeros((ATTN_HD, blk), F32)
            for j in range(i + 1):
                p = jnp.exp(s_scr[j] - mx)
                den = den + jnp.sum(p, axis=0, keepdims=True)
                acc = acc + _dot(vt_ref[hh * ATTN_HD:(hh + 1) * ATTN_HD, j * blk:(j + 1) * blk],
                                 p.astype(BF16))
            outs.append(acc / den)
        o_ref[0, i * blk:(i + 1) * blk, :] = jnp.concatenate(outs, axis=0).T.astype(o_ref.dtype)


def _moba(qa, ka, vat, B, S):
    aw = ATTN_HEADS * ATTN_HD
    nb = S // MOBA_BLOCK
    slopes = 2.0 ** (-8.0 * jnp.arange(1, ATTN_HEADS + 1, dtype=F32) / ATTN_HEADS)
    seq = pl.BlockSpec((1, S, LANES), lambda b, h: (b, 0, h))
    return pl.pallas_call(
        functools.partial(_moba_kernel, nb=nb),
        grid=(B, ATTN_HEADS // 2),
        in_specs=[pl.BlockSpec(memory_space=pltpu.SMEM), seq, seq,
                  pl.BlockSpec((LANES, S), lambda b, h: (h, b))],
        out_specs=seq,
        out_shape=jax.ShapeDtypeStruct((B, S, aw), BF16),
        scratch_shapes=[pltpu.VMEM((nb, MOBA_BLOCK, MOBA_BLOCK), F32)],
        compiler_params=_params("parallel", "parallel"),
        name="moba",
    )(slopes, qa.reshape(B, S, aw), ka.reshape(B, S, aw), vat)


def _hgrn_kernel(q_ref, f_ref, i_ref, g_ref, it_ref, lb_ref, ng_ref, o_ref, st_scr, lvl_scr, *, nchunks):
    C = REC_CHUNK
    row = lax.broadcasted_iota(jnp.int32, (C, REC_D), 0)
    rr = lax.broadcasted_iota(jnp.int32, (C, C), 0)
    cc = lax.broadcasted_iota(jnp.int32, (C, C), 1)
    lvl = jnp.zeros((C, C), jnp.int32)
    for c in REC_LEVELS:
        sh_c = c.bit_length() - 1
        rb_c = jnp.right_shift(rr, sh_c)
        hit = ((rb_c & 1) == 1) & (jnp.right_shift(cc, sh_c) == rb_c - 1)
        lvl = jnp.where(hit, c, lvl)
    lvl_scr[...] = lvl
    st_scr[...] = jnp.zeros_like(st_scr)
    lb = lb_ref[...]
    ng = ng_ref[...]

    def chunk(ci, carry):
        r0 = pl.multiple_of(ci * C, C)
        qc = q_ref[0, pl.ds(r0, C), :]
        fc = f_ref[0, pl.ds(r0, C), :]
        vc = i_ref[0, pl.ds(r0, C), :]
        gc = g_ref[0, pl.ds(r0, C), :]
        f = lb + (1.0 - lb) * _sigmoid(fc)
        kk = 1.0 - f
        qs = qc * _sigmoid(qc)
        cum = jnp.log(f)
        sh = 1
        while sh < C:
            cum = cum + jnp.where(row >= sh, pltpu.roll(cum, sh, axis=0), 0.0)
            sh *= 2
        tot = cum[C - 1:C, :]
        vb = vc.astype(BF16)
        state_t = st_scr[...]
        out = _dot_nt((qs * jnp.exp(cum)).astype(BF16), state_t.astype(BF16))
        k_end = (kk * jnp.exp(tot - cum)).astype(BF16)
        st_scr[...] = state_t * jnp.exp(tot) + _dot(it_ref[:, pl.ds(r0, C)], k_end)
        lvl_m = lvl_scr[...]
        a_mat = jnp.zeros((C, C), F32)
        for c in REC_LEVELS:
            nblk = C // c
            ends = [cum[(b + 1) * c - 1:(b + 1) * c, :] for b in range(nblk)]
            r_q = jnp.zeros((C, REC_D), F32)
            r_k = jnp.broadcast_to(ends[0], (C, REC_D))
            for b in range(1, nblk):
                r_q = jnp.where(row >= b * c, ends[b - 1], r_q)
                r_k = jnp.where(row >= b * c, ends[b], r_k)
            qd = qs * jnp.exp(cum - r_q)
            kd = kk * jnp.exp(r_k - cum)
            a_c = _dot_nt(qd.astype(BF16), kd.astype(BF16))
            a_mat = jnp.where(lvl_m == c, a_c, a_mat)
        out = out + _dot(a_mat.astype(BF16), vb)
        pos = row & (REC_BAND - 1)
        for d in range(REC_BAND):
            if d == 0:
                y = qs * kk
                vs = vc
            else:
                ks = pltpu.roll(kk, d, axis=0)
                cs = pltpu.roll(cum, d, axis=0)
                vs = pltpu.roll(vc, d, axis=0)
                y = qs * ks * jnp.exp(jnp.minimum(cum - cs, 0.0))
                y = jnp.where(pos >= d, y, 0.0)
            out = out + jnp.sum(y, axis=-1, keepdims=True) * vs
        out = _rms(out, ng) * (gc * _sigmoid(gc))
        o_ref[0, pl.ds(r0, C), :] = out.astype(o_ref.dtype)
        return carry

    lax.fori_loop(0, nchunks, chunk, 0)


def _hgrn(qr, fr, ir, gr, irt, lb, ng, B, S):
    rw = REC_HEADS * REC_D
    seq = pl.BlockSpec((1, S, REC_D), lambda b, h: (b, 0, h))
    vec = pl.BlockSpec((1, REC_D), lambda b, h: (0, h))
    r3 = lambda a: a.reshape(B, S, rw)
    return pl.pallas_call(
        functools.partial(_hgrn_kernel, nchunks=S // REC_CHUNK),
        grid=(B, REC_HEADS),
        in_specs=[seq, seq, seq, seq, pl.BlockSpec((REC_D, S), lambda b, h: (h, b)), vec, vec],
        out_specs=seq,
        out_shape=jax.ShapeDtypeStruct((B, S, rw), BF16),
        scratch_shapes=[pltpu.VMEM((REC_D, REC_D), F32), pltpu.VMEM((REC_CHUNK, REC_CHUNK), jnp.int32)],
        compiler_params=_params("parallel", "parallel"),
        name="hgrn2",
    )(r3(qr), r3(fr), r3(ir), r3(gr), irt, lb.reshape(1, rw), ng.reshape(1, rw))


def _out_proj_kernel(x_ref, a_ref, r_ref, wa_ref, wr_ref, g_ref, wq_ref, h_ref, hnt_ref, qp_ref):
    h = x_ref[...] + _dot(a_ref[...], wa_ref[...]) + _dot(r_ref[...], wr_ref[...])
    h_ref[...] = h
    hn = _rms(h, g_ref[...])
    hnt_ref[...] = hn.T.astype(BF16)
    qp_ref[...] = _dot(hn.astype(BF16), wq_ref[...]).astype(BF16)


def _out_proj(x2, attn, rec, w_out, g2, wq, tm):
    T, D = x2.shape
    aw = attn.shape[1]
    rw = rec.shape[1]
    wo = w_out.astype(BF16)
    wqb = wq.astype(BF16)
    nq = wqb.shape[1]
    tok = lambda n: pl.BlockSpec((tm, n), lambda i: (i, 0))
    full = lambda a: pl.BlockSpec(a.shape, lambda i: (0, 0))
    ops = (x2, attn, rec, wo[:aw], wo[aw:], g2.reshape(1, D), wqb)
    return pl.pallas_call(
        _out_proj_kernel,
        grid=(T // tm,),
        in_specs=[tok(D), tok(aw), tok(rw)] + [full(a) for a in ops[3:]],
        out_specs=[tok(D), pl.BlockSpec((D, tm), lambda i: (0, i)), tok(nq)],
        out_shape=[jax.ShapeDtypeStruct((T, D), F32), jax.ShapeDtypeStruct((D, T), BF16),
                   jax.ShapeDtypeStruct((T, nq), BF16)],
        compiler_params=_params("parallel"),
        name="out_proj",
    )(*ops)


def _extract_desc(x, n):
    out = []
    prev = None
    for _ in range(n):
        cur = x if prev is None else jnp.where(x < prev, x, -jnp.inf)
        prev = jnp.max(cur, axis=0, keepdims=True)
        out.append(prev)
    return out


def _peer_topk_kernel(qp_ref, sk_ref, a_ref, e2_ref, phi_ref, v_scr):
    k = PEER_TOPK
    v_scr[...] = jnp.full(v_scr.shape, -jnp.inf, F32)
    s_both = []
    for c in range(2):
        s_t = _dot_nt(sk_ref[0, c], qp_ref[:, c * PEER_NKEYS:(c + 1) * PEER_NKEYS])
        s_both.append(s_t)
        for r, v in enumerate(_extract_desc(s_t, k + 1)):
            v_scr[c, r:r + 1, :] = v
    v1 = v_scr[0]
    v2 = v_scr[1]
    cands = [v2 + v1[0:1]]
    cands += [v2[0:8] + v1[a:a + 1] for a in range(1, 8)]
    cands += [v1[8:] + v2[0:1]]
    taus = _extract_desc(jnp.concatenate(cands, axis=0), k + 1)
    z = jnp.zeros_like(taus[0])
    for r in range(k):
        z = z + jnp.exp(taus[r] - taus[0])
    tmid = 0.5 * (taus[k - 1] + taus[k])
    m1, m2 = v1[0:1], v2[0:1]
    scale = GELU_SCALE / z
    a_ref[0] = jnp.exp(s_both[0] - m1) * scale
    e2_ref[0] = jnp.exp(s_both[1] - m2).astype(e2_ref.dtype)
    phi_ref[0] = jnp.broadcast_to(jnp.exp(tmid - (m1 + m2)) * scale, phi_ref.shape[1:])


def _peer_topk(qp, subkeys, tb):
    T = qp.shape[0]
    dk = 2 * PEER_NKEYS
    skb = subkeys.astype(BF16)
    tile = pl.BlockSpec((1, PEER_NKEYS, tb), lambda i, h: (h, 0, i))
    return pl.pallas_call(
        _peer_topk_kernel,
        grid=(T // tb, PEER_HEADS),
        in_specs=[pl.BlockSpec((tb, dk), lambda i, h: (i, h)),
                  pl.BlockSpec((1, 2, PEER_NKEYS, dk // 2), lambda i, h: (h, 0, 0, 0))],
        out_specs=[tile, tile, pl.BlockSpec((1, 8, tb), lambda i, h: (h, 0, i))],
        out_shape=[jax.ShapeDtypeStruct((PEER_HEADS, PEER_NKEYS, T), F32),
                   jax.ShapeDtypeStruct((PEER_HEADS, PEER_NKEYS, T), BF16),
                   jax.ShapeDtypeStruct((PEER_HEADS, 8, T), F32)],
        scratch_shapes=[pltpu.VMEM((2, PEER_TOPK + 8, tb), F32)],
        compiler_params=_params("parallel", "parallel"),
        name="peer_topk",
    )(qp, skb)


PEER_ACT_CHUNKS = 2


def _peer_gate_chunk(j, s, c, a_ref, e2_ref, phi_ref, act_scr, p_scr, *, tb, et):
    nk = PEER_NKEYS
    rb = 32
    n1 = et // nk
    per = n1 // PEER_ACT_CHUNKS
    i1_0 = pl.multiple_of((2 * j + s) * n1, n1)
    rows = per * nk
    act_v = act_scr.at[pl.ds(pl.multiple_of(c * rows, rows), rows), :]
    p_v = p_scr.at[pl.ds(pl.multiple_of(c * rows, rows), rows), :]
    for lg in range(tb // LANES):
        ls = slice(lg * LANES, (lg + 1) * LANES)
        phi = [phi_ref[h, 0:1, ls].astype(BF16) for h in range(PEER_HEADS)]
        a_all = [a_ref[h, pl.ds(i1_0, n1), ls] for h in range(PEER_HEADS)]
        for q in range(per):
            shift = (n1 - (c * per + q)) % n1
            aa = [pltpu.roll(a, shift, axis=0)[0:1, :].astype(BF16) for a in a_all]
            for r in range(nk // rb):
                rs = slice(r * rb, (r + 1) * rb)
                es = slice(q * nk + r * rb, q * nk + (r + 1) * rb)
                w = jnp.zeros((rb, LANES), BF16)
                for h in range(PEER_HEADS):
                    y = aa[h] * e2_ref[h, rs, ls]
                    w = w + jnp.where(y >= phi[h], y, jnp.zeros_like(y))
                z = act_v[es, ls]
                p_v[es, ls] = w * (z * (1.0 + lax.erf(z))).astype(BF16)


def _peer_main_kernel(hnt_ref, u_ref, un_ref, vt_ref, a_ref, e2_ref, phi_ref, o_ref, acta_scr, actb_scr, pa_scr,
                      pb_scr, vk_scr, tmp_scr, *, tb, et):
    j = pl.program_id(1)
    nc = PEER_ACT_CHUNKS
    cr = et // nc
    dr = o_ref.shape[0] // nc

    @pl.when(j == 0)
    def _():
        o_ref[...] = jnp.zeros_like(o_ref)
        tmp_scr[...] = jnp.zeros_like(tmp_scr)
        pb_scr[...] = jnp.zeros_like(pb_scr)
        vk_scr[...] = jnp.zeros_like(vk_scr)

    act_bufs = (acta_scr, actb_scr)

    def act(s, c, src=None):
        rows = pl.ds(pl.multiple_of(c * cr, cr), cr)
        lhs = u_ref.at[s * et:(s + 1) * et, :][rows, :] if src is None else src[rows, :]
        act_bufs[s][rows, :] = _dot(lhs, hnt_ref[...])

    def gate(s, c):
        _peer_gate_chunk(j, s, c, a_ref, e2_ref, phi_ref, act_bufs[s], pa_scr if s == 0 else pb_scr, tb=tb, et=et)

    def value(s, c):
        rows = pl.ds(pl.multiple_of(c * dr, dr), dr)
        o_ref[rows, :] += tmp_scr[1 - s, rows, :]
        if s == 0:
            tmp_scr[0, rows, :] = _dot(vk_scr[rows, :], pb_scr[...])
        else:
            tmp_scr[1, rows, :] = _dot(vt_ref[rows, 0:et], pa_scr[...])

    half = nc // 2

    @pl.when(j == 0)
    def _():
        for c in range(half):
            act(0, c)

    def trip(s, s_act, d_act, src=None):
        def body(c, carry):
            gate(s, c)
            act(s_act, c + d_act, src)
            value(s, c)
            return carry
        return body

    lax.fori_loop(0, half, trip(0, 0, half), 0)
    lax.fori_loop(half, nc, trip(0, 1, -half), 0)
    lax.fori_loop(0, half, trip(1, 1, half), 0)
    lax.fori_loop(half, nc, trip(1, 0, -half, un_ref), 0)
    vk_scr[...] = vt_ref[:, et:2 * et]

    @pl.when(j == pl.num_programs(1) - 1)
    def _():
        o_ref[...] += tmp_scr[1] + _dot(vk_scr[...], pb_scr[...])


def _peer_main(hnt, u, v, a_t, e2_t, phi, tb, et):
    D, T = hnt.shape
    N = u.shape[0]
    ub = (u * GELU_SCALE).astype(BF16)
    vtb = v.astype(BF16).T
    un = et // 2
    sc = pl.BlockSpec((PEER_HEADS, PEER_NKEYS, tb), lambda i, j: (0, 0, i))
    return pl.pallas_call(
        functools.partial(_peer_main_kernel, tb=tb, et=et),
        grid=(T // tb, N // (2 * et)),
        in_specs=[pl.BlockSpec((D, tb), lambda i, j: (0, i)),
                  pl.BlockSpec((2 * et, D), lambda i, j: (j, 0)),
                  pl.BlockSpec((un, D), lambda i, j: (jnp.minimum((j + 1) * (2 * et // un), N // un - 1), 0)),
                  pl.BlockSpec((D, 2 * et), lambda i, j: (0, j)),
                  sc, sc, pl.BlockSpec((PEER_HEADS, 8, tb), lambda i, j: (0, 0, i))],
        out_specs=pl.BlockSpec((D, tb), lambda i, j: (0, i)),
        out_shape=jax.ShapeDtypeStruct((D, T), F32),
        scratch_shapes=[pltpu.VMEM((et, tb), F32), pltpu.VMEM((et, tb), F32),
                        pltpu.VMEM((et, tb), BF16), pltpu.VMEM((et, tb), BF16),
                        pltpu.VMEM((D, et), BF16), pltpu.VMEM((2, D, tb), F32)],
        compiler_params=_params("parallel", "arbitrary"),
        name="peer_main",
    )(hnt, ub, ub, vtb, a_t, e2_t, phi)


def _final_kernel(pt_ref, h_ref, g_ref, o_ref):
    o_ref[...] = _rms(h_ref[...] + pt_ref[...].T, g_ref[...])


def _final(peer_t, h1, g, tm):
    T, D = h1.shape
    return pl.pallas_call(
        _final_kernel,
        grid=(T // tm,),
        in_specs=[pl.BlockSpec((D, tm), lambda i: (0, i)), pl.BlockSpec((tm, D), lambda i: (i, 0)),
                  pl.BlockSpec((1, D), lambda i: (0, 0))],
        out_specs=pl.BlockSpec((tm, D), lambda i: (i, 0)),
        out_shape=jax.ShapeDtypeStruct((T, D), F32),
        compiler_params=_params("parallel"),
        name="final",
    )(peer_t, h1, g.reshape(1, D))


def _residual_kernel(pt_ref, h_ref, o_ref):
    o_ref[...] = h_ref[...] + pt_ref[...].T


def _residual(peer_t, h1, tm):
    T, D = h1.shape
    return pl.pallas_call(
        _residual_kernel,
        grid=(T // tm,),
        in_specs=[pl.BlockSpec((D, tm), lambda i: (0, i)), pl.BlockSpec((tm, D), lambda i: (i, 0))],
        out_specs=pl.BlockSpec((tm, D), lambda i: (i, 0)),
        out_shape=jax.ShapeDtypeStruct((T, D), F32),
        compiler_params=_params("parallel"),
        name="residual",
    )(peer_t, h1)


def _tile(n, want):
    t = min(n, want)
    assert n % t == 0, (n, t)
    return t


def kernel(x, norm1_g, w_in, rec_lb_logits, rec_norm_g, w_out, norm2_g, peer_wq, peer_subkeys, peer_u,
           peer_v, normf_g):
    B, S, D = x.shape
    T = B * S
    depth = norm1_g.shape[0]
    assert S % MOBA_BLOCK == 0 and S % REC_CHUNK == 0
    tm = _tile(T, 512)
    tb_topk = _tile(T, 1024)
    tb_peer = _tile(T, 512)
    et = 8 * PEER_NKEYS
    lb_all = jnp.cumsum(jax.nn.softmax(rec_lb_logits.astype(F32), axis=0), axis=0)
    h = x.reshape(T, D)
    out = None
    for layer in range(depth):
        qa, ka, vat, qr, fr, ir, gr, irt = _in_proj(h, norm1_g[layer], w_in[layer], tm)
        attn = _moba(qa, ka, vat, B, S).reshape(T, -1)
        rec = _hgrn(qr, fr, ir, gr, irt, lb_all[layer], rec_norm_g[layer], B, S).reshape(T, -1)
        h1, hnt, qp = _out_proj(h, attn, rec, w_out[layer], norm2_g[layer], peer_wq[layer], tm)
        a_t, e2_t, phi = _peer_topk(qp, peer_subkeys[layer], tb_topk)
        peer_t = _peer_main(hnt, peer_u[layer], peer_v[layer], a_t, e2_t, phi, tb_peer, et)
        if layer == depth - 1:
            out = _final(peer_t, h1, normf_g, tm)
        else:
            h = _residual(peer_t, h1, tm)
    return out.reshape(B, S, D)
```

```python
import functools

import jax
import jax.numpy as jnp
import numpy as np
from jax import lax
from jax.experimental import pallas as pl
from jax.experimental.pallas import tpu as pltpu

F32 = jnp.float32
BF16 = jnp.bfloat16
F8 = jnp.float8_e4m3fn

NORM_EPS = 1e-6
ATTN_HEADS = 8
ATTN_HD = 64
MOBA_BLOCK = 256
MOBA_TOPK = 3
REC_HEADS = 4
REC_D = 128
PEER_HEADS = 8
PEER_NKEYS = 128
PEER_TOPK = 16
REC_CHUNK = 128
REC_BAND = 8
REC_LEVELS = tuple(c for c in (64, 32, 16, 8) if c >= REC_BAND)
LANES = 128
GELU_SCALE = 2.0 ** -0.5
PEER_U_SCALE = 32.0
PEER_P_SCALE = 16.0
PEER_V_SCALE = 8.0
VMEM_LIMIT = 56 * 1024 * 1024

_NT = (((1,), (1,)), ((), ()))


def _dot(a, b):
    return jnp.dot(a, b, preferred_element_type=F32)


def _dot_nt(a, b):
    return lax.dot_general(a, b, _NT, preferred_element_type=F32)


def _sigmoid(x):
    return 1.0 / (1.0 + jnp.exp(-x))


def _rms(x, g):
    return x * lax.rsqrt(jnp.mean(x * x, axis=-1, keepdims=True) + NORM_EPS) * g


def _params(*sem):
    return pltpu.CompilerParams(dimension_semantics=sem, vmem_limit_bytes=VMEM_LIMIT)


def _in_proj_kernel(x_ref, g_ref, wq_ref, wk_ref, wvt_ref, wr_ref, wit_ref,
                    qa_ref, ka_ref, vat_ref, qr_ref, fr_ref, ir_ref, gr_ref, irt_ref):
    xn = _rms(x_ref[...], g_ref[...]).astype(BF16)
    qa_ref[...] = _dot(xn, wq_ref[...]).astype(BF16)
    ka_ref[...] = _dot(xn, wk_ref[...]).astype(BF16)
    vat_ref[...] = _dot_nt(wvt_ref[...], xn).astype(BF16)
    w = wr_ref.shape[1] // 4
    qr_ref[...] = _dot(xn, wr_ref[:, 0 * w:1 * w])
    fr_ref[...] = _dot(xn, wr_ref[:, 1 * w:2 * w])
    ir_ref[...] = _dot(xn, wr_ref[:, 2 * w:3 * w])
    gr_ref[...] = _dot(xn, wr_ref[:, 3 * w:4 * w])
    irt_ref[...] = _dot_nt(wit_ref[...], xn).astype(BF16)


def _in_proj(x2, g, w_in, tm):
    T, D = x2.shape
    aw = ATTN_HEADS * ATTN_HD
    rw = REC_HEADS * REC_D
    wb = w_in.astype(BF16)
    wq, wk, wv = wb[:, 0:aw], wb[:, aw:2 * aw], wb[:, 2 * aw:3 * aw]
    wr = wb[:, 3 * aw:]
    wi = wb[:, 3 * aw + 2 * rw:3 * aw + 3 * rw]
    tok = lambda n: pl.BlockSpec((tm, n), lambda i: (i, 0))
    tokt = lambda n: pl.BlockSpec((n, tm), lambda i: (0, i))
    full = lambda a: pl.BlockSpec(a.shape, lambda i: (0, 0))
    ops = (x2, g.reshape(1, D), wq, wk, wv.T, wr, wi.T)
    return pl.pallas_call(
        _in_proj_kernel,
        grid=(T // tm,),
        in_specs=[tok(D)] + [full(a) for a in ops[1:]],
        out_specs=[tok(aw), tok(aw), tokt(aw), tok(rw), tok(rw), tok(rw), tok(rw), tokt(rw)],
        out_shape=[jax.ShapeDtypeStruct((T, aw), BF16), jax.ShapeDtypeStruct((T, aw), BF16),
                   jax.ShapeDtypeStruct((aw, T), BF16),
                   jax.ShapeDtypeStruct((T, rw), F32), jax.ShapeDtypeStruct((T, rw), F32),
                   jax.ShapeDtypeStruct((T, rw), F32), jax.ShapeDtypeStruct((T, rw), F32),
                   jax.ShapeDtypeStruct((rw, T), BF16)],
        compiler_params=_params("parallel"),
        name="in_proj",
    )(*ops)


def _moba_kernel(slopes_ref, q_ref, k_ref, vt_ref, o_ref, s_scr, *, nb):
    blk = MOBA_BLOCK
    hp = pl.program_id(1)
    neg = -jnp.inf
    lane = lax.broadcasted_iota(jnp.int32, (1, LANES), 1)
    rel = (lax.broadcasted_iota(jnp.int32, (blk, blk), 1)
           - lax.broadcasted_iota(jnp.int32, (blk, blk), 0)).astype(F32)
    kmean = jnp.concatenate(
        [jnp.mean(k_ref[0, j * blk:(j + 1) * blk, :].astype(F32), axis=0, keepdims=True)
         for j in range(nb)], axis=0)
    heads = []
    for hh in range(2):
        in_head = (lane >= hh * ATTN_HD) & (lane < (hh + 1) * ATTN_HD)
        km = jnp.where(in_head, kmean, 0.0)
        km_hi = km.astype(BF16)
        km_lo = (km - km_hi.astype(F32)).astype(BF16)
        qmask = jnp.where(in_head, ATTN_HD ** -0.5, 0.0).astype(BF16)
        heads.append((slopes_ref[2 * hp + hh], km_hi, km_lo, qmask))
    for i in range(nb):
        q_i = q_ref[0, i * blk:(i + 1) * blk, :]
        outs = []
        for hh in range(2):
            slope, km_hi, km_lo, qmask = heads[hh]
            qs = q_i * qmask
            sel = None
            if i > MOBA_TOPK:
                gate = _dot_nt(km_hi, qs) + _dot_nt(km_lo, qs)
                rows = [gate[j:j + 1, :] for j in range(i)]
                sel = []
                for j in range(i):
                    rank = jnp.zeros((1, blk), F32)
                    for m in range(i):
                        if m == j:
                            continue
                        beats = (rows[m] >= rows[j]) if m < j else (rows[m] > rows[j])
                        rank = rank + jnp.where(beats, 1.0, 0.0)
                    sel.append(rank < MOBA_TOPK)
            mx = jnp.full((1, blk), neg, F32)
            for j in range(i + 1):
                st = _dot_nt(k_ref[0, j * blk:(j + 1) * blk, :], qs)
                st = st - slope * (rel + float((i - j) * blk))
                if j == i:
                    st = jnp.where(rel >= 0.0, st, neg)
                elif sel is not None:
                    st = jnp.where(sel[j], st, neg)
                s_scr[j] = st
                mx = jnp.maximum(mx, jnp.max(st, axis=0, keepdims=True))
            den = jnp.zeros((1, blk), F32)
            acc = jnp.zeros((ATTN_HD, blk), F32)
            for j in range(i + 1):
                p = jnp.exp(s_scr[j] - mx)
                den = den + jnp.sum(p, axis=0, keepdims=True)
                acc = acc + _dot(vt_ref[hh * ATTN_HD:(hh + 1) * ATTN_HD, j * blk:(j + 1) * blk],
                                 p.astype(BF16))
            outs.append(acc / den)
        o_ref[0, i * blk:(i + 1) * blk, :] = jnp.concatenate(outs, axis=0).T.astype(o_ref.dtype)


def _moba(qa, ka, vat, B, S):
    aw = ATTN_HEADS * ATTN_HD
    nb = S // MOBA_BLOCK
    slopes = 2.0 ** (-8.0 * jnp.arange(1, ATTN_HEADS + 1, dtype=F32) / ATTN_HEADS)
    seq = pl.BlockSpec((1, S, LANES), lambda b, h: (b, 0, h))
    return pl.pallas_call(
        functools.partial(_moba_kernel, nb=nb),
        grid=(B, ATTN_HEADS // 2),
        in_specs=[pl.BlockSpec(memory_space=pltpu.SMEM), seq, seq,
                  pl.BlockSpec((LANES, S), lambda b, h: (h, b))],
        out_specs=seq,
        out_shape=jax.ShapeDtypeStruct((B, S, aw), BF16),
        scratch_shapes=[pltpu.VMEM((nb, MOBA_BLOCK, MOBA_BLOCK), F32)],
        compiler_params=_params("parallel", "parallel"),
        name="moba",
    )(slopes, qa.reshape(B, S, aw), ka.reshape(B, S, aw), vat)


def _hgrn_kernel(q_ref, f_ref, i_ref, g_ref, it_ref, lb_ref, ng_ref, o_ref, st_scr, lvl_scr, *, nchunks):
    C = REC_CHUNK
    row = lax.broadcasted_iota(jnp.int32, (C, REC_D), 0)
    rr = lax.broadcasted_iota(jnp.int32, (C, C), 0)
    cc = lax.broadcasted_iota(jnp.int32, (C, C), 1)
    lvl = jnp.zeros((C, C), jnp.int32)
    for c in REC_LEVELS:
        sh_c = c.bit_length() - 1
        rb_c = jnp.right_shift(rr, sh_c)
        hit = ((rb_c & 1) == 1) & (jnp.right_shift(cc, sh_c) == rb_c - 1)
        lvl = jnp.where(hit, c, lvl)
    lvl_scr[...] = lvl
    st_scr[...] = jnp.zeros_like(st_scr)
    lb = lb_ref[...]
    ng = ng_ref[...]

    def chunk(ci, carry):
        r0 = pl.multiple_of(ci * C, C)
        qc = q_ref[0, pl.ds(r0, C), :]
        fc = f_ref[0, pl.ds(r0, C), :]
        vc = i_ref[0, pl.ds(r0, C), :]
        gc = g_ref[0, pl.ds(r0, C), :]
        f = lb + (1.0 - lb) * _sigmoid(fc)
        kk = 1.0 - f
        qs = qc * _sigmoid(qc)
        cum = jnp.log(f)
        sh = 1
        while sh < C:
            cum = cum + jnp.where(row >= sh, pltpu.roll(cum, sh, axis=0), 0.0)
            sh *= 2
        tot = cum[C - 1:C, :]
        vb = vc.astype(BF16)
        state_t = st_scr[...]
        out = _dot_nt((qs * jnp.exp(cum)).astype(BF16), state_t.astype(BF16))
        k_end = (kk * jnp.exp(tot - cum)).astype(BF16)
        st_scr[...] = state_t * jnp.exp(tot) + _dot(it_ref[:, pl.ds(r0, C)], k_end)
        lvl_m = lvl_scr[...]
        a_mat = jnp.zeros((C, C), F32)
        for c in REC_LEVELS:
            nblk = C // c
            ends = [cum[(b + 1) * c - 1:(b + 1) * c, :] for b in range(nblk)]
            r_q = jnp.zeros((C, REC_D), F32)
            r_k = jnp.broadcast_to(ends[0], (C, REC_D))
            for b in range(1, nblk):
                r_q = jnp.where(row >= b * c, ends[b - 1], r_q)
                r_k = jnp.where(row >= b * c, ends[b], r_k)
            qd = qs * jnp.exp(cum - r_q)
            kd = kk * jnp.exp(r_k - cum)
            a_c = _dot_nt(qd.astype(BF16), kd.astype(BF16))
            a_mat = jnp.where(lvl_m == c, a_c, a_mat)
        out = out + _dot(a_mat.astype(BF16), vb)
        pos = row & (REC_BAND - 1)
        for d in range(REC_BAND):
            if d == 0:
                y = qs * kk
                vs = vc
            else:
                ks = pltpu.roll(kk, d, axis=0)
                cs = pltpu.roll(cum, d, axis=0)
                vs = pltpu.roll(vc, d, axis=0)
                y = qs * ks * jnp.exp(jnp.minimum(cum - cs, 0.0))
                y = jnp.where(pos >= d, y, 0.0)
            out = out + jnp.sum(y, axis=-1, keepdims=True) * vs
        out = _rms(out, ng) * (gc * _sigmoid(gc))
        o_ref[0, pl.ds(r0, C), :] = out.astype(o_ref.dtype)
        return carry

    lax.fori_loop(0, nchunks, chunk, 0)


def _hgrn(qr, fr, ir, gr, irt, lb, ng, B, S):
    rw = REC_HEADS * REC_D
    seq = pl.BlockSpec((1, S, REC_D), lambda b, h: (b, 0, h))
    vec = pl.BlockSpec((1, REC_D), lambda b, h: (0, h))
    r3 = lambda a: a.reshape(B, S, rw)
    return pl.pallas_call(
        functools.partial(_hgrn_kernel, nchunks=S // REC_CHUNK),
        grid=(B, REC_HEADS),
        in_specs=[seq, seq, seq, seq, pl.BlockSpec((REC_D, S), lambda b, h: (h, b)), vec, vec],
        out_specs=seq,
        out_shape=jax.ShapeDtypeStruct((B, S, rw), BF16),
        scratch_shapes=[pltpu.VMEM((REC_D, REC_D), F32), pltpu.VMEM((REC_CHUNK, REC_CHUNK), jnp.int32)],
        compiler_params=_params("parallel", "parallel"),
        name="hgrn2",
    )(r3(qr), r3(fr), r3(ir), r3(gr), irt, lb.reshape(1, rw), ng.reshape(1, rw))


def _out_proj_kernel(x_ref, a_ref, r_ref, wa_ref, wr_ref, g_ref, wq_ref, h_ref, hnt_ref, qp_ref):
    h = x_ref[...] + _dot(a_ref[...], wa_ref[...]) + _dot(r_ref[...], wr_ref[...])
    h_ref[...] = h
    hn = _rms(h, g_ref[...])
    hnt_ref[...] = hn.T.astype(hnt_ref.dtype)
    qp_ref[...] = _dot(hn.astype(BF16), wq_ref[...]).astype(BF16)


def _out_proj(x2, attn, rec, w_out, g2, wq, tm):
    T, D = x2.shape
    aw = attn.shape[1]
    rw = rec.shape[1]
    wo = w_out.astype(BF16)
    wqb = wq.astype(BF16)
    nq = wqb.shape[1]
    tok = lambda n: pl.BlockSpec((tm, n), lambda i: (i, 0))
    full = lambda a: pl.BlockSpec(a.shape, lambda i: (0, 0))
    ops = (x2, attn, rec, wo[:aw], wo[aw:], g2.reshape(1, D), wqb)
    return pl.pallas_call(
        _out_proj_kernel,
        grid=(T // tm,),
        in_specs=[tok(D), tok(aw), tok(rw)] + [full(a) for a in ops[3:]],
        out_specs=[tok(D), pl.BlockSpec((D, tm), lambda i: (0, i)), tok(nq)],
        out_shape=[jax.ShapeDtypeStruct((T, D), F32), jax.ShapeDtypeStruct((D, T), F8),
                   jax.ShapeDtypeStruct((T, nq), BF16)],
        compiler_params=_params("parallel"),
        name="out_proj",
    )(*ops)


def _extract_desc(x, n):
    out = []
    prev = None
    for _ in range(n):
        cur = x if prev is None else jnp.where(x < prev, x, -jnp.inf)
        prev = jnp.max(cur, axis=0, keepdims=True)
        out.append(prev)
    return out


def _peer_topk_kernel(qp_ref, sk_ref, a_ref, e2_ref, phi_ref, v_scr):
    k = PEER_TOPK
    v_scr[...] = jnp.full(v_scr.shape, -jnp.inf, F32)
    s_both = []
    for c in range(2):
        s_t = _dot_nt(sk_ref[0, c], qp_ref[:, c * PEER_NKEYS:(c + 1) * PEER_NKEYS])
        s_both.append(s_t)
        for r, v in enumerate(_extract_desc(s_t, k + 1)):
            v_scr[c, r:r + 1, :] = v
    v1 = v_scr[0]
    v2 = v_scr[1]
    cands = [v2 + v1[0:1]]
    cands += [v2[0:8] + v1[a:a + 1] for a in range(1, 8)]
    cands += [v1[8:] + v2[0:1]]
    taus = _extract_desc(jnp.concatenate(cands, axis=0), k + 1)
    z = jnp.zeros_like(taus[0])
    for r in range(k):
        z = z + jnp.exp(taus[r] - taus[0])
    tmid = 0.5 * (taus[k - 1] + taus[k])
    m1, m2 = v1[0:1], v2[0:1]
    scale = (GELU_SCALE * PEER_P_SCALE) / z
    a_ref[0] = jnp.exp(s_both[0] - m1) * scale
    e2_ref[0] = jnp.exp(s_both[1] - m2).astype(e2_ref.dtype)
    phi_ref[0] = jnp.broadcast_to(jnp.exp(tmid - (m1 + m2)) * scale, phi_ref.shape[1:])


def _peer_topk(qp, subkeys, tb):
    T = qp.shape[0]
    dk = 2 * PEER_NKEYS
    skb = subkeys.astype(BF16)
    tile = pl.BlockSpec((1, PEER_NKEYS, tb), lambda i, h: (h, 0, i))
    return pl.pallas_call(
        _peer_topk_kernel,
        grid=(T // tb, PEER_HEADS),
        in_specs=[pl.BlockSpec((tb, dk), lambda i, h: (i, h)),
                  pl.BlockSpec((1, 2, PEER_NKEYS, dk // 2), lambda i, h: (h, 0, 0, 0))],
        out_specs=[tile, tile, pl.BlockSpec((1, 8, tb), lambda i, h: (h, 0, i))],
        out_shape=[jax.ShapeDtypeStruct((PEER_HEADS, PEER_NKEYS, T), F32),
                   jax.ShapeDtypeStruct((PEER_HEADS, PEER_NKEYS, T), BF16),
                   jax.ShapeDtypeStruct((PEER_HEADS, 8, T), F32)],
        scratch_shapes=[pltpu.VMEM((2, PEER_TOPK + 8, tb), F32)],
        compiler_params=_params("parallel", "parallel"),
        name="peer_topk",
    )(qp, skb)


PEER_ACT_CHUNKS = 2


def _peer_gate_chunk(j, s, c, a_ref, e2_ref, phi_ref, act_scr, p_scr, *, tb, et):
    nk = PEER_NKEYS
    rb = 32
    n1 = et // nk
    per = n1 // PEER_ACT_CHUNKS
    i1_0 = pl.multiple_of((2 * j + s) * n1, n1)
    rows = per * nk
    act_v = act_scr.at[pl.ds(pl.multiple_of(c * rows, rows), rows), :]
    p_v = p_scr.at[pl.ds(pl.multiple_of(c * rows, rows), rows), :]
    for lg in range(tb // LANES):
        ls = slice(lg * LANES, (lg + 1) * LANES)
        phi = [phi_ref[h, 0:1, ls].astype(BF16) for h in range(PEER_HEADS)]
        a_all = [a_ref[h, pl.ds(i1_0, n1), ls] for h in range(PEER_HEADS)]
        for q in range(per):
            shift = (n1 - (c * per + q)) % n1
            aa = [pltpu.roll(a, shift, axis=0)[0:1, :].astype(BF16) for a in a_all]
            for r in range(nk // rb):
                rs = slice(r * rb, (r + 1) * rb)
                es = slice(q * nk + r * rb, q * nk + (r + 1) * rb)
                w = jnp.zeros((rb, LANES), BF16)
                for h in range(PEER_HEADS):
                    y = aa[h] * e2_ref[h, rs, ls]
                    w = w + jnp.where(y >= phi[h], y, jnp.zeros_like(y))
                z = act_v[es, ls] * (1.0 / PEER_U_SCALE)
                p_v[es, ls] = (w * (z * (1.0 + lax.erf(z))).astype(BF16)).astype(p_v.dtype)


def _peer_main_kernel(hnt_ref, u_ref, un_ref, vt_ref, a_ref, e2_ref, phi_ref, o_ref, acta_scr, actb_scr, pa_scr,
                      pb_scr, vk_scr, tmp_scr, *, tb, et):
    j = pl.program_id(1)
    nc = PEER_ACT_CHUNKS
    cr = et // nc
    dr = o_ref.shape[0] // nc

    @pl.when(j == 0)
    def _():
        o_ref[...] = jnp.zeros_like(o_ref)
        tmp_scr[...] = jnp.zeros_like(tmp_scr)
        pb_scr[...] = jnp.zeros_like(pb_scr)
        vk_scr[...] = jnp.zeros_like(vk_scr)

    act_bufs = (acta_scr, actb_scr)

    def act(s, c, src=None):
        rows = pl.ds(pl.multiple_of(c * cr, cr), cr)
        lhs = u_ref.at[s * et:(s + 1) * et, :][rows, :] if src is None else src[rows, :]
        act_bufs[s][rows, :] = _dot(lhs, hnt_ref[...])

    def gate(s, c):
        _peer_gate_chunk(j, s, c, a_ref, e2_ref, phi_ref, act_bufs[s], pa_scr if s == 0 else pb_scr, tb=tb, et=et)

    def value(s, c):
        rows = pl.ds(pl.multiple_of(c * dr, dr), dr)
        o_ref[rows, :] += tmp_scr[1 - s, rows, :]
        if s == 0:
            tmp_scr[0, rows, :] = _dot(vk_scr[rows, :], pb_scr[...])
        else:
            tmp_scr[1, rows, :] = _dot(vt_ref[rows, 0:et], pa_scr[...])

    half = nc // 2

    @pl.when(j == 0)
    def _():
        for c in range(half):
            act(0, c)

    def trip(s, s_act, d_act, src=None):
        def body(c, carry):
            gate(s, c)
            act(s_act, c + d_act, src)
            value(s, c)
            return carry
        return body

    lax.fori_loop(0, half, trip(0, 0, half), 0)
    lax.fori_loop(half, nc, trip(0, 1, -half), 0)
    lax.fori_loop(0, half, trip(1, 1, half), 0)
    lax.fori_loop(half, nc, trip(1, 0, -half, un_ref), 0)
    vk_scr[...] = vt_ref[:, et:2 * et]

    @pl.when(j == pl.num_programs(1) - 1)
    def _():
        o_ref[...] += tmp_scr[1] + _dot(vk_scr[...], pb_scr[...])


def _peer_main(hnt, u, v, a_t, e2_t, phi, tb, et):
    D, T = hnt.shape
    N = u.shape[0]
    ub = (u * (GELU_SCALE * PEER_U_SCALE)).astype(F8)
    vtb = (v * PEER_V_SCALE).astype(F8).T
    un = et // 2
    sc = pl.BlockSpec((PEER_HEADS, PEER_NKEYS, tb), lambda i, j: (0, 0, i))
    return pl.pallas_call(
        functools.partial(_peer_main_kernel, tb=tb, et=et),
        grid=(T // tb, N // (2 * et)),
        in_specs=[pl.BlockSpec((D, tb), lambda i, j: (0, i)),
                  pl.BlockSpec((2 * et, D), lambda i, j: (j, 0)),
                  pl.BlockSpec((un, D), lambda i, j: (jnp.minimum((j + 1) * (2 * et // un), N // un - 1), 0)),
                  pl.BlockSpec((D, 2 * et), lambda i, j: (0, j)),
                  sc, sc, pl.BlockSpec((PEER_HEADS, 8, tb), lambda i, j: (0, 0, i))],
        out_specs=pl.BlockSpec((D, tb), lambda i, j: (0, i)),
        out_shape=jax.ShapeDtypeStruct((D, T), F32),
        scratch_shapes=[pltpu.VMEM((et, tb), F32), pltpu.VMEM((et, tb), F32),
                        pltpu.VMEM((et, tb), F8), pltpu.VMEM((et, tb), F8),
                        pltpu.VMEM((D, et), F8), pltpu.VMEM((2, D, tb), F32)],
        compiler_params=_params("parallel", "arbitrary"),
        name="peer_main",
    )(hnt, ub, ub, vtb, a_t, e2_t, phi)


def _final_kernel(pt_ref, h_ref, g_ref, o_ref):
    o_ref[...] = _rms(h_ref[...] + pt_ref[...].T * (1.0 / (PEER_P_SCALE * PEER_V_SCALE)), g_ref[...])


def _final(peer_t, h1, g, tm):
    T, D = h1.shape
    return pl.pallas_call(
        _final_kernel,
        grid=(T // tm,),
        in_specs=[pl.BlockSpec((D, tm), lambda i: (0, i)), pl.BlockSpec((tm, D), lambda i: (i, 0)),
                  pl.BlockSpec((1, D), lambda i: (0, 0))],
        out_specs=pl.BlockSpec((tm, D), lambda i: (i, 0)),
        out_shape=jax.ShapeDtypeStruct((T, D), F32),
        compiler_params=_params("parallel"),
        name="final",
    )(peer_t, h1, g.reshape(1, D))


def _residual_kernel(pt_ref, h_ref, o_ref):
    o_ref[...] = h_ref[...] + pt_ref[...].T * (1.0 / (PEER_P_SCALE * PEER_V_SCALE))


def _residual(peer_t, h1, tm):
    T, D = h1.shape
    return pl.pallas_call(
        _residual_kernel,
        grid=(T // tm,),
        in_specs=[pl.BlockSpec((D, tm), lambda i: (0, i)), pl.BlockSpec((tm, D), lambda i: (i, 0))],
        out_specs=pl.BlockSpec((tm, D), lambda i: (i, 0)),
        out_shape=jax.ShapeDtypeStruct((T, D), F32),
        compiler_params=_params("parallel"),
        name="residual",
    )(peer_t, h1)


def _tile(n, want):
    t = min(n, want)
    assert n % t == 0, (n, t)
    return t


def kernel(x, norm1_g, w_in, rec_lb_logits, rec_norm_g, w_out, norm2_g, peer_wq, peer_subkeys, peer_u,
           peer_v, normf_g):
    B, S, D = x.shape
    T = B * S
    depth = norm1_g.shape[0]
    assert S % MOBA_BLOCK == 0 and S % REC_CHUNK == 0
    tm = _tile(T, 512)
    tb_topk = _tile(T, 2048)
    tb_peer = _tile(T, 512)
    et = 8 * PEER_NKEYS
    lb_all = jnp.cumsum(jax.nn.softmax(rec_lb_logits.astype(F32), axis=0), axis=0)
    h = x.reshape(T, D)
    out = None
    for layer in range(depth):
        qa, ka, vat, qr, fr, ir, gr, irt = _in_proj(h, norm1_g[layer], w_in[layer], tm)
        attn = _moba(qa, ka, vat, B, S).reshape(T, -1)
        rec = _hgrn(qr, fr, ir, gr, irt, lb_all[layer], rec_norm_g[layer], B, S).reshape(T, -1)
        h1, hnt, qp = _out_proj(h, attn, rec, w_out[layer], norm2_g[layer], peer_wq[layer], tm)
        a_t, e2_t, phi = _peer_topk(qp, peer_subkeys[layer], tb_topk)
        peer_t = _peer_main(hnt, peer_u[layer], peer_v[layer], a_t, e2_t, phi, tb_peer, et)
        if layer == depth - 1:
            out = _final(peer_t, h1, normf_g, tm)
        else:
            h = _residual(peer_t, h1, tm)
    return out.reshape(B, S, D)
```

```python
import functools

import jax
import jax.numpy as jnp
import numpy as np
from jax import lax
from jax.experimental import pallas as pl
from jax.experimental.pallas import tpu as pltpu

F32 = jnp.float32
BF16 = jnp.bfloat16
F8 = jnp.float8_e4m3fn

NORM_EPS = 1e-6
ATTN_HEADS = 8
ATTN_HD = 64
MOBA_BLOCK = 256
MOBA_TOPK = 3
REC_HEADS = 4
REC_D = 128
PEER_HEADS = 8
PEER_NKEYS = 128
PEER_TOPK = 16
REC_CHUNK = 128
REC_BAND = 8
REC_LEVELS = tuple(c for c in (64, 32, 16, 8) if c >= REC_BAND)
LANES = 128
GELU_SCALE = 2.0 ** -0.5
PEER_U_SCALE = 32.0
PEER_P_SCALE = 16.0
PEER_V_SCALE = 8.0
VMEM_LIMIT = 56 * 1024 * 1024

_NT = (((1,), (1,)), ((), ()))


def _dot(a, b):
    return jnp.dot(a, b, preferred_element_type=F32)


def _dot_nt(a, b):
    return lax.dot_general(a, b, _NT, preferred_element_type=F32)


def _sigmoid(x):
    return 1.0 / (1.0 + jnp.exp(-x))


def _rms(x, g):
    return x * lax.rsqrt(jnp.mean(x * x, axis=-1, keepdims=True) + NORM_EPS) * g


def _params(*sem):
    return pltpu.CompilerParams(dimension_semantics=sem, vmem_limit_bytes=VMEM_LIMIT)


def _in_proj_kernel(x_ref, g_ref, wq_ref, wk_ref, wvt_ref, wr_ref, wit_ref,
                    qa_ref, ka_ref, vat_ref, qr_ref, fr_ref, ir_ref, gr_ref, irt_ref):
    xn = _rms(x_ref[...], g_ref[...]).astype(BF16)
    qa_ref[...] = _dot(xn, wq_ref[...]).astype(BF16)
    ka_ref[...] = _dot(xn, wk_ref[...]).astype(BF16)
    vat_ref[...] = _dot_nt(wvt_ref[...], xn).astype(BF16)
    w = wr_ref.shape[1] // 4
    qr_ref[...] = _dot(xn, wr_ref[:, 0 * w:1 * w])
    fr_ref[...] = _dot(xn, wr_ref[:, 1 * w:2 * w])
    ir_ref[...] = _dot(xn, wr_ref[:, 2 * w:3 * w])
    gr_ref[...] = _dot(xn, wr_ref[:, 3 * w:4 * w])
    irt_ref[...] = _dot_nt(wit_ref[...], xn).astype(BF16)


def _in_proj(x2, g, w_in, tm):
    T, D = x2.shape
    aw = ATTN_HEADS * ATTN_HD
    rw = REC_HEADS * REC_D
    wb = w_in.astype(BF16)
    wq, wk, wv = wb[:, 0:aw], wb[:, aw:2 * aw], wb[:, 2 * aw:3 * aw]
    wr = wb[:, 3 * aw:]
    wi = wb[:, 3 * aw + 2 * rw:3 * aw + 3 * rw]
    tok = lambda n: pl.BlockSpec((tm, n), lambda i: (i, 0))
    tokt = lambda n: pl.BlockSpec((n, tm), lambda i: (0, i))
    full = lambda a: pl.BlockSpec(a.shape, lambda i: (0, 0))
    ops = (x2, g.reshape(1, D), wq, wk, wv.T, wr, wi.T)
    return pl.pallas_call(
        _in_proj_kernel,
        grid=(T // tm,),
        in_specs=[tok(D)] + [full(a) for a in ops[1:]],
        out_specs=[tok(aw), tok(aw), tokt(aw), tok(rw), tok(rw), tok(rw), tok(rw), tokt(rw)],
        out_shape=[jax.ShapeDtypeStruct((T, aw), BF16), jax.ShapeDtypeStruct((T, aw), BF16),
                   jax.ShapeDtypeStruct((aw, T), BF16),
                   jax.ShapeDtypeStruct((T, rw), F32), jax.ShapeDtypeStruct((T, rw), F32),
                   jax.ShapeDtypeStruct((T, rw), F32), jax.ShapeDtypeStruct((T, rw), F32),
                   jax.ShapeDtypeStruct((rw, T), BF16)],
        compiler_params=_params("parallel"),
        name="in_proj",
    )(*ops)


def _moba_kernel(slopes_ref, q_ref, k_ref, vt_ref, o_ref, s_scr, *, nb):
    blk = MOBA_BLOCK
    hp = pl.program_id(1)
    neg = -jnp.inf
    lane = lax.broadcasted_iota(jnp.int32, (1, LANES), 1)
    rel = (lax.broadcasted_iota(jnp.int32, (blk, blk), 1)
           - lax.broadcasted_iota(jnp.int32, (blk, blk), 0)).astype(F32)
    kmean = jnp.concatenate(
        [jnp.mean(k_ref[0, j * blk:(j + 1) * blk, :].astype(F32), axis=0, keepdims=True)
         for j in range(nb)], axis=0)
    heads = []
    for hh in range(2):
        in_head = (lane >= hh * ATTN_HD) & (lane < (hh + 1) * ATTN_HD)
        km = jnp.where(in_head, kmean, 0.0)
        km_hi = km.astype(BF16)
        km_lo = (km - km_hi.astype(F32)).astype(BF16)
        qmask = jnp.where(in_head, ATTN_HD ** -0.5, 0.0).astype(BF16)
        heads.append((slopes_ref[2 * hp + hh], km_hi, km_lo, qmask))
    for i in range(nb):
        q_i = q_ref[0, i * blk:(i + 1) * blk, :]
        outs = []
        for hh in range(2):
            slope, km_hi, km_lo, qmask = heads[hh]
            qs = q_i * qmask
            sel = None
            if i > MOBA_TOPK:
                gate = _dot_nt(km_hi, qs) + _dot_nt(km_lo, qs)
                rows = [gate[j:j + 1, :] for j in range(i)]
                sel = []
                for j in range(i):
                    rank = jnp.zeros((1, blk), F32)
                    for m in range(i):
                        if m == j:
                            continue
                        beats = (rows[m] >= rows[j]) if m < j else (rows[m] > rows[j])
                        rank = rank + jnp.where(beats, 1.0, 0.0)
                    sel.append(rank < MOBA_TOPK)
            mx = jnp.full((1, blk), neg, F32)
            for j in range(i + 1):
                st = _dot_nt(k_ref[0, j * blk:(j + 1) * blk, :], qs)
                st = st - slope * (rel + float((i - j) * blk))
                if j == i:
                    st = jnp.where(rel >= 0.0, st, neg)
                elif sel is not None:
                    st = jnp.where(sel[j], st, neg)
                s_scr[j] = st
                mx = jnp.maximum(mx, jnp.max(st, axis=0, keepdims=True))
            den = jnp.zeros((1, blk), F32)
            acc = jnp.zeros((ATTN_HD, blk), F32)
            for j in range(i + 1):
                p = jnp.exp(s_scr[j] - mx)
                den = den + jnp.sum(p, axis=0, keepdims=True)
                acc = acc + _dot(vt_ref[hh * ATTN_HD:(hh + 1) * ATTN_HD, j * blk:(j + 1) * blk],
                                 p.astype(BF16))
            outs.append(acc / den)
        o_ref[0, i * blk:(i + 1) * blk, :] = jnp.concatenate(outs, axis=0).T.astype(o_ref.dtype)


def _moba(qa, ka, vat, B, S):
    aw = ATTN_HEADS * ATTN_HD
    nb = S // MOBA_BLOCK
    slopes = 2.0 ** (-8.0 * jnp.arange(1, ATTN_HEADS + 1, dtype=F32) / ATTN_HEADS)
    seq = pl.BlockSpec((1, S, LANES), lambda b, h: (b, 0, h))
    return pl.pallas_call(
        functools.partial(_moba_kernel, nb=nb),
        grid=(B, ATTN_HEADS // 2),
        in_specs=[pl.BlockSpec(memory_space=pltpu.SMEM), seq, seq,
                  pl.BlockSpec((LANES, S), lambda b, h: (h, b))],
        out_specs=seq,
        out_shape=jax.ShapeDtypeStruct((B, S, aw), BF16),
        scratch_shapes=[pltpu.VMEM((nb, MOBA_BLOCK, MOBA_BLOCK), F32)],
        compiler_params=_params("parallel", "parallel"),
        name="moba",
    )(slopes, qa.reshape(B, S, aw), ka.reshape(B, S, aw), vat)


def _hgrn_kernel(q_ref, f_ref, i_ref, g_ref, it_ref, lb_ref, ng_ref, o_ref, st_scr, lvl_scr, *, nchunks):
    C = REC_CHUNK
    row = lax.broadcasted_iota(jnp.int32, (C, REC_D), 0)
    rr = lax.broadcasted_iota(jnp.int32, (C, C), 0)
    cc = lax.broadcasted_iota(jnp.int32, (C, C), 1)
    lvl = jnp.zeros((C, C), jnp.int32)
    for c in REC_LEVELS:
        sh_c = c.bit_length() - 1
        rb_c = jnp.right_shift(rr, sh_c)
        hit = ((rb_c & 1) == 1) & (jnp.right_shift(cc, sh_c) == rb_c - 1)
        lvl = jnp.where(hit, c, lvl)
    lvl_scr[...] = lvl
    st_scr[...] = jnp.zeros_like(st_scr)
    lb = lb_ref[...]
    ng = ng_ref[...]

    def chunk(ci, carry):
        r0 = pl.multiple_of(ci * C, C)
        qc = q_ref[0, pl.ds(r0, C), :]
        fc = f_ref[0, pl.ds(r0, C), :]
        vc = i_ref[0, pl.ds(r0, C), :]
        gc = g_ref[0, pl.ds(r0, C), :]
        f = lb + (1.0 - lb) * _sigmoid(fc)
        kk = 1.0 - f
        qs = qc * _sigmoid(qc)
        cum = jnp.log(f)
        sh = 1
        while sh < C:
            cum = cum + jnp.where(row >= sh, pltpu.roll(cum, sh, axis=0), 0.0)
            sh *= 2
        tot = cum[C - 1:C, :]
        vb = vc.astype(BF16)
        state_t = st_scr[...]
        out = _dot_nt((qs * jnp.exp(cum)).astype(BF16), state_t.astype(BF16))
        k_end = (kk * jnp.exp(tot - cum)).astype(BF16)
        st_scr[...] = state_t * jnp.exp(tot) + _dot(it_ref[:, pl.ds(r0, C)], k_end)
        lvl_m = lvl_scr[...]
        a_mat = jnp.zeros((C, C), F32)
        for c in REC_LEVELS:
            nblk = C // c
            ends = [cum[(b + 1) * c - 1:(b + 1) * c, :] for b in range(nblk)]
            r_q = jnp.zeros((C, REC_D), F32)
            r_k = jnp.broadcast_to(ends[0], (C, REC_D))
            for b in range(1, nblk):
                r_q = jnp.where(row >= b * c, ends[b - 1], r_q)
                r_k = jnp.where(row >= b * c, ends[b], r_k)
            qd = qs * jnp.exp(cum - r_q)
            kd = kk * jnp.exp(r_k - cum)
            a_c = _dot_nt(qd.astype(BF16), kd.astype(BF16))
            a_mat = jnp.where(lvl_m == c, a_c, a_mat)
        out = out + _dot(a_mat.astype(BF16), vb)
        pos = row & (REC_BAND - 1)
        for d in range(REC_BAND):
            if d == 0:
                y = qs * kk
                vs = vc
            else:
                ks = pltpu.roll(kk, d, axis=0)
                cs = pltpu.roll(cum, d, axis=0)
                vs = pltpu.roll(vc, d, axis=0)
                y = qs * ks * jnp.exp(jnp.minimum(cum - cs, 0.0))
                y = jnp.where(pos >= d, y, 0.0)
            out = out + jnp.sum(y, axis=-1, keepdims=True) * vs
        out = _rms(out, ng) * (gc * _sigmoid(gc))
        o_ref[0, pl.ds(r0, C), :] = out.astype(o_ref.dtype)
        return carry

    lax.fori_loop(0, nchunks, chunk, 0)


def _hgrn(qr, fr, ir, gr, irt, lb, ng, B, S):
    rw = REC_HEADS * REC_D
    seq = pl.BlockSpec((1, S, REC_D), lambda b, h: (b, 0, h))
    vec = pl.BlockSpec((1, REC_D), lambda b, h: (0, h))
    r3 = lambda a: a.reshape(B, S, rw)
    return pl.pallas_call(
        functools.partial(_hgrn_kernel, nchunks=S // REC_CHUNK),
        grid=(B, REC_HEADS),
        in_specs=[seq, seq, seq, seq, pl.BlockSpec((REC_D, S), lambda b, h: (h, b)), vec, vec],
        out_specs=seq,
        out_shape=jax.ShapeDtypeStruct((B, S, rw), BF16),
        scratch_shapes=[pltpu.VMEM((REC_D, REC_D), F32), pltpu.VMEM((REC_CHUNK, REC_CHUNK), jnp.int32)],
        compiler_params=_params("parallel", "parallel"),
        name="hgrn2",
    )(r3(qr), r3(fr), r3(ir), r3(gr), irt, lb.reshape(1, rw), ng.reshape(1, rw))


def _out_proj_kernel(x_ref, a_ref, r_ref, wa_ref, wr_ref, g_ref, wq_ref, h_ref, hnt_ref, qp_ref):
    h = x_ref[...] + _dot(a_ref[...], wa_ref[...]) + _dot(r_ref[...], wr_ref[...])
    h_ref[...] = h
    hn = _rms(h, g_ref[...])
    hnt_ref[...] = hn.T.astype(hnt_ref.dtype)
    qp_ref[...] = _dot(hn.astype(BF16), wq_ref[...]).astype(BF16)


def _out_proj(x2, attn, rec, w_out, g2, wq, tm):
    T, D = x2.shape
    aw = attn.shape[1]
    rw = rec.shape[1]
    wo = w_out.astype(BF16)
    wqb = wq.astype(BF16)
    nq = wqb.shape[1]
    tok = lambda n: pl.BlockSpec((tm, n), lambda i: (i, 0))
    full = lambda a: pl.BlockSpec(a.shape, lambda i: (0, 0))
    ops = (x2, attn, rec, wo[:aw], wo[aw:], g2.reshape(1, D), wqb)
    return pl.pallas_call(
        _out_proj_kernel,
        grid=(T // tm,),
        in_specs=[tok(D), tok(aw), tok(rw)] + [full(a) for a in ops[3:]],
        out_specs=[tok(D), pl.BlockSpec((D, tm), lambda i: (0, i)), tok(nq)],
        out_shape=[jax.ShapeDtypeStruct((T, D), F32), jax.ShapeDtypeStruct((D, T), F8),
                   jax.ShapeDtypeStruct((T, nq), BF16)],
        compiler_params=_params("parallel"),
        name="out_proj",
    )(*ops)


def _extract_desc(x, n):
    out = []
    prev = None
    for _ in range(n):
        cur = x if prev is None else jnp.where(x < prev, x, -jnp.inf)
        prev = jnp.max(cur, axis=0, keepdims=True)
        out.append(prev)
    return out


def _peer_topk_kernel(qp_ref, sk_ref, a_ref, e2_ref, phi_ref, v_scr):
    k = PEER_TOPK
    v_scr[...] = jnp.full(v_scr.shape, -jnp.inf, F32)
    s_both = []
    for c in range(2):
        s_t = _dot_nt(sk_ref[0, c], qp_ref[:, c * PEER_NKEYS:(c + 1) * PEER_NKEYS])
        s_both.append(s_t)
        for r, v in enumerate(_extract_desc(s_t, k + 1)):
            v_scr[c, r:r + 1, :] = v
    v1 = v_scr[0]
    v2 = v_scr[1]
    cands = [v2 + v1[0:1]]
    cands += [v2[0:8] + v1[a:a + 1] for a in range(1, 8)]
    cands += [v1[8:] + v2[0:1]]
    taus = _extract_desc(jnp.concatenate(cands, axis=0), k + 1)
    z = jnp.zeros_like(taus[0])
    for r in range(k):
        z = z + jnp.exp(taus[r] - taus[0])
    tmid = 0.5 * (taus[k - 1] + taus[k])
    m1, m2 = v1[0:1], v2[0:1]
    scale = (GELU_SCALE * PEER_P_SCALE) / z
    a_ref[0] = jnp.exp(s_both[0] - m1) * scale
    e2_ref[0] = jnp.exp(s_both[1] - m2).astype(e2_ref.dtype)
    phi_ref[0] = jnp.broadcast_to(jnp.exp(tmid - (m1 + m2)) * scale, phi_ref.shape[1:])


def _peer_topk(qp, subkeys, tb):
    T = qp.shape[0]
    dk = 2 * PEER_NKEYS
    skb = subkeys.astype(BF16)
    tile = pl.BlockSpec((1, PEER_NKEYS, tb), lambda i, h: (h, 0, i))
    return pl.pallas_call(
        _peer_topk_kernel,
        grid=(T // tb, PEER_HEADS),
        in_specs=[pl.BlockSpec((tb, dk), lambda i, h: (i, h)),
                  pl.BlockSpec((1, 2, PEER_NKEYS, dk // 2), lambda i, h: (h, 0, 0, 0))],
        out_specs=[tile, tile, pl.BlockSpec((1, 8, tb), lambda i, h: (h, 0, i))],
        out_shape=[jax.ShapeDtypeStruct((PEER_HEADS, PEER_NKEYS, T), F32),
                   jax.ShapeDtypeStruct((PEER_HEADS, PEER_NKEYS, T), BF16),
                   jax.ShapeDtypeStruct((PEER_HEADS, 8, T), F32)],
        scratch_shapes=[pltpu.VMEM((2, PEER_TOPK + 8, tb), F32)],
        compiler_params=_params("parallel", "parallel"),
        name="peer_topk",
    )(qp, skb)


PEER_ACT_CHUNKS = 2


def _peer_gate_chunk(j, s, c, a_ref, e2_ref, phi_ref, act_scr, p_scr, *, tb, et):
    nk = PEER_NKEYS
    rb = 32
    n1 = et // nk
    per = n1 // PEER_ACT_CHUNKS
    i1_0 = pl.multiple_of((2 * j + s) * n1, n1)
    rows = per * nk
    act_v = act_scr.at[pl.ds(pl.multiple_of(c * rows, rows), rows), :]
    p_v = p_scr.at[pl.ds(pl.multiple_of(c * rows, rows), rows), :]
    for lg in range(tb // LANES):
        ls = slice(lg * LANES, (lg + 1) * LANES)
        phi = [phi_ref[h, 0:1, ls].astype(BF16) for h in range(PEER_HEADS)]
        a_all = [a_ref[h, pl.ds(i1_0, n1), ls] for h in range(PEER_HEADS)]
        for q in range(per):
            shift = (n1 - (c * per + q)) % n1
            aa = [pltpu.roll(a, shift, axis=0)[0:1, :].astype(BF16) for a in a_all]
            for r in range(nk // rb):
                rs = slice(r * rb, (r + 1) * rb)
                es = slice(q * nk + r * rb, q * nk + (r + 1) * rb)
                w = jnp.zeros((rb, LANES), BF16)
                for h in range(PEER_HEADS):
                    y = aa[h] * e2_ref[h, rs, ls]
                    w = w + jnp.where(y >= phi[h], y, jnp.zeros_like(y))
                z = act_v[es, ls] * (1.0 / PEER_U_SCALE)
                p_v[es, ls] = (w * (z * (1.0 + lax.erf(z))).astype(BF16)).astype(p_v.dtype)


def _peer_main_kernel(hnt_ref, u_ref, un_ref, vt_ref, a_ref, e2_ref, phi_ref, h1_ref, g_ref, out_ref, acta_scr,
                      actb_scr, pa_scr, pb_scr, vk_scr, tmp_scr, o_ref, *, tb, et, final_norm):
    j = pl.program_id(1)
    nc = PEER_ACT_CHUNKS
    cr = et // nc
    dr = o_ref.shape[0] // nc

    @pl.when(j == 0)
    def _():
        o_ref[...] = jnp.zeros_like(o_ref)
        tmp_scr[...] = jnp.zeros_like(tmp_scr)
        pb_scr[...] = jnp.zeros_like(pb_scr)
        vk_scr[...] = jnp.zeros_like(vk_scr)

    act_bufs = (acta_scr, actb_scr)

    def act(s, c, src=None):
        rows = pl.ds(pl.multiple_of(c * cr, cr), cr)
        lhs = u_ref.at[s * et:(s + 1) * et, :][rows, :] if src is None else src[rows, :]
        act_bufs[s][rows, :] = _dot(lhs, hnt_ref[...])

    def gate(s, c):
        _peer_gate_chunk(j, s, c, a_ref, e2_ref, phi_ref, act_bufs[s], pa_scr if s == 0 else pb_scr, tb=tb, et=et)

    def value(s, c):
        rows = pl.ds(pl.multiple_of(c * dr, dr), dr)
        o_ref[rows, :] += tmp_scr[1 - s, rows, :]
        if s == 0:
            tmp_scr[0, rows, :] = _dot(vk_scr[rows, :], pb_scr[...])
        else:
            tmp_scr[1, rows, :] = _dot(vt_ref[rows, 0:et], pa_scr[...])

    half = nc // 2

    @pl.when(j == 0)
    def _():
        for c in range(half):
            act(0, c)

    def trip(s, s_act, d_act, src=None):
        def body(c, carry):
            gate(s, c)
            act(s_act, c + d_act, src)
            value(s, c)
            return carry
        return body

    lax.fori_loop(0, half, trip(0, 0, half), 0)
    lax.fori_loop(half, nc, trip(0, 1, -half), 0)
    lax.fori_loop(0, half, trip(1, 1, half), 0)
    lax.fori_loop(half, nc, trip(1, 0, -half, un_ref), 0)
    vk_scr[...] = vt_ref[:, et:2 * et]

    @pl.when(j == pl.num_programs(1) - 1)
    def _():
        o_ref[...] += tmp_scr[1] + _dot(vk_scr[...], pb_scr[...])
        h = h1_ref[...] + o_ref[...].T * (1.0 / (PEER_P_SCALE * PEER_V_SCALE))
        out_ref[...] = _rms(h, g_ref[...]) if final_norm else h


def _peer_main(hnt, u, v, a_t, e2_t, phi, h1, g, final_norm, tb, et):
    D, T = hnt.shape
    N = u.shape[0]
    ub = (u * (GELU_SCALE * PEER_U_SCALE)).astype(F8)
    vtb = (v * PEER_V_SCALE).astype(F8).T
    un = et // 2
    sc = pl.BlockSpec((PEER_HEADS, PEER_NKEYS, tb), lambda i, j: (0, 0, i))
    return pl.pallas_call(
        functools.partial(_peer_main_kernel, tb=tb, et=et, final_norm=final_norm),
        grid=(T // tb, N // (2 * et)),
        in_specs=[pl.BlockSpec((D, tb), lambda i, j: (0, i)),
                  pl.BlockSpec((2 * et, D), lambda i, j: (j, 0)),
                  pl.BlockSpec((un, D), lambda i, j: (jnp.minimum((j + 1) * (2 * et // un), N // un - 1), 0)),
                  pl.BlockSpec((D, 2 * et), lambda i, j: (0, j)),
                  sc, sc, pl.BlockSpec((PEER_HEADS, 8, tb), lambda i, j: (0, 0, i)),
                  pl.BlockSpec((tb, D), lambda i, j: (i, 0)), pl.BlockSpec((1, D), lambda i, j: (0, 0))],
        out_specs=pl.BlockSpec((tb, D), lambda i, j: (i, 0)),
        out_shape=jax.ShapeDtypeStruct((T, D), F32),
        scratch_shapes=[pltpu.VMEM((et, tb), F32), pltpu.VMEM((et, tb), F32),
                        pltpu.VMEM((et, tb), F8), pltpu.VMEM((et, tb), F8),
                        pltpu.VMEM((D, et), F8), pltpu.VMEM((2, D, tb), F32), pltpu.VMEM((D, tb), F32)],
        compiler_params=_params("parallel", "arbitrary"),
        name="peer_main",
    )(hnt, ub, ub, vtb, a_t, e2_t, phi, h1, g.reshape(1, D))


def _final_kernel(pt_ref, h_ref, g_ref, o_ref):
    o_ref[...] = _rms(h_ref[...] + pt_ref[...].T * (1.0 / (PEER_P_SCALE * PEER_V_SCALE)), g_ref[...])


def _final(peer_t, h1, g, tm):
    T, D = h1.shape
    return pl.pallas_call(
        _final_kernel,
        grid=(T // tm,),
        in_specs=[pl.BlockSpec((D, tm), lambda i: (0, i)), pl.BlockSpec((tm, D), lambda i: (i, 0)),
                  pl.BlockSpec((1, D), lambda i: (0, 0))],
        out_specs=pl.BlockSpec((tm, D), lambda i: (i, 0)),
        out_shape=jax.ShapeDtypeStruct((T, D), F32),
        compiler_params=_params("parallel"),
        name="final",
    )(peer_t, h1, g.reshape(1, D))


def _residual_kernel(pt_ref, h_ref, o_ref):
    o_ref[...] = h_ref[...] + pt_ref[...].T * (1.0 / (PEER_P_SCALE * PEER_V_SCALE))


def _residual(peer_t, h1, tm):
    T, D = h1.shape
    return pl.pallas_call(
        _residual_kernel,
        grid=(T // tm,),
        in_specs=[pl.BlockSpec((D, tm), lambda i: (0, i)), pl.BlockSpec((tm, D), lambda i: (i, 0))],
        out_specs=pl.BlockSpec((tm, D), lambda i: (i, 0)),
        out_shape=jax.ShapeDtypeStruct((T, D), F32),
        compiler_params=_params("parallel"),
        name="residual",
    )(peer_t, h1)


def _tile(n, want):
    t = min(n, want)
    assert n % t == 0, (n, t)
    return t


def kernel(x, norm1_g, w_in, rec_lb_logits, rec_norm_g, w_out, norm2_g, peer_wq, peer_subkeys, peer_u,
           peer_v, normf_g):
    B, S, D = x.shape
    T = B * S
    depth = norm1_g.shape[0]
    assert S % MOBA_BLOCK == 0 and S % REC_CHUNK == 0
    tm = _tile(T, 512)
    tb_topk = _tile(T, 2048)
    tb_peer = _tile(T, 512)
    et = 8 * PEER_NKEYS
    lb_all = jnp.cumsum(jax.nn.softmax(rec_lb_logits.astype(F32), axis=0), axis=0)
    h = x.reshape(T, D)
    out = None
    for layer in range(depth):
        qa, ka, vat, qr, fr, ir, gr, irt = _in_proj(h, norm1_g[layer], w_in[layer], tm)
        attn = _moba(qa, ka, vat, B, S).reshape(T, -1)
        rec = _hgrn(qr, fr, ir, gr, irt, lb_all[layer], rec_norm_g[layer], B, S).reshape(T, -1)
        h1, hnt, qp = _out_proj(h, attn, rec, w_out[layer], norm2_g[layer], peer_wq[layer], tm)
        a_t, e2_t, phi = _peer_topk(qp, peer_subkeys[layer], tb_topk)
        last = layer == depth - 1
        h = _peer_main(hnt, peer_u[layer], peer_v[layer], a_t, e2_t, phi, h1, normf_g, last, tb_peer, et)
    return h.reshape(B, S, D)
```
